```python
import math
import jax
import jax.numpy as jnp
from jax import lax
import numpy as np

D_MODEL = 1024
BATCH = 4
SEQ = 8192
DEPTH = 1
DEC_BATCH = 128
DEC_SEQ = 1
PAST_LEN = 8192
PAGE_SIZE = 128

HEAD_DIM = 64
N_HEADS = 8
D_ATT = N_HEADS * HEAD_DIM
D_CONV = D_MODEL - D_ATT
D_MIX = D_ATT + D_CONV
DILATED_CFGS = ((128, 1), (512, 4), (2048, 16))
WIN_MAX = max(w for w, _ in DILATED_CFGS)
ATT_BLK = 128
CONV_W = 31
NUM_BUCKETS = 32
MAX_DISTANCE = WIN_MAX
N_GROUPS = 4
EXPERTS_PER_GROUP = 8
N_EXPERTS = N_GROUPS * EXPERTS_PER_GROUP
TOP_K = 2
D_EXPERT = 512
EXPERT_BLK = 128
EPS = 1e-6
NEG_INF = -1e30
D_IN = 3 * D_ATT + 2 * D_CONV

kernel_name = 'hybrid_dilated_attn_conformer_conv_hmoe_step'


def rmsnorm(x, g):
    xf = x.astype(jnp.float32)
    y = xf * lax.rsqrt(jnp.mean(xf * xf, axis=-1, keepdims=True) + EPS)
    return (y * g.astype(jnp.float32)).astype(x.dtype)


def layernorm(x, g, b):
    xf = x.astype(jnp.float32)
    xc = xf - jnp.mean(xf, axis=-1, keepdims=True)
    var = jnp.mean(xc * xc, axis=-1, keepdims=True)
    return (xc * lax.rsqrt(var + EPS) * g.astype(jnp.float32) + b.astype(jnp.float32)).astype(x.dtype)


def t5_bucket(dist):
    n = jnp.maximum(dist, 0)
    max_exact = NUM_BUCKETS // 2
    nf = jnp.maximum(n, 1).astype(jnp.float32)
    large = max_exact + (jnp.log(nf / max_exact) / math.log(MAX_DISTANCE / max_exact)
                         * (NUM_BUCKETS - max_exact)).astype(jnp.int32)
    large = jnp.minimum(large, NUM_BUCKETS - 1)
    return jnp.where(n < max_exact, n, large)


def banded_dilated_branch(q, k, v, rel_bias, window, dilation):
    B, S, H, Dh = q.shape
    wsub = window // dilation
    N = S // dilation
    nb = -(-N // ATT_BLK)
    Np = nb * ATT_BLK

    def to_sub(t):
        return t.reshape(B, N, dilation, H, Dh).transpose(0, 2, 3, 1, 4)

    qs = jnp.pad(to_sub(q), ((0, 0), (0, 0), (0, 0), (0, Np - N), (0, 0)))
    kp = jnp.pad(to_sub(k), ((0, 0), (0, 0), (0, 0), (ATT_BLK, Np - N), (0, 0)))
    vp = jnp.pad(to_sub(v), ((0, 0), (0, 0), (0, 0), (ATT_BLK, Np - N), (0, 0)))

    def band(t):
        cur = t[:, :, :, ATT_BLK:].reshape(B, dilation, H, nb, ATT_BLK, Dh)
        prev = t[:, :, :, :Np].reshape(B, dilation, H, nb, ATT_BLK, Dh)
        return jnp.concatenate([prev, cur], axis=4)

    kb, vb = band(kp), band(vp)
    qb = qs.reshape(B, dilation, H, nb, ATT_BLK, Dh)
    i = jnp.arange(ATT_BLK)[:, None]
    j = jnp.arange(2 * ATT_BLK)[None, :]
    rel = i + ATT_BLK - j
    bias = rel_bias[t5_bucket(rel * dilation)].astype(jnp.float32).transpose(2, 0, 1)
    kpos = jnp.arange(nb)[:, None, None] * ATT_BLK - ATT_BLK + j[None]
    valid = (rel >= 0) & (rel <= wsub) & (kpos >= 0)
    s = jnp.einsum('bdhnqe,bdhnke->bdhnqk', qb, kb, preferred_element_type=jnp.float32)
    s = jnp.where(valid[None, None, None], s + bias[None, None, :, None], NEG_INF)
    m = jnp.max(s, axis=-1)
    p = jnp.exp(s - m[..., None])
    l = jnp.sum(p, axis=-1)
    num = jnp.einsum('bdhnqk,bdhnke->bdhnqe', p, vb.astype(jnp.float32))
    num = num.reshape(B, dilation, H, Np, Dh)[:, :, :, :N].transpose(0, 3, 1, 2, 4).reshape(B, S, H, Dh)

    def back(t):
        return t.reshape(B, dilation, H, Np)[:, :, :, :N].transpose(0, 3, 1, 2).reshape(B, S, H)

    return num, back(m), back(l)


def gathered_dilated_branch(q, kext, vext, rel_bias, window, dilation, past_len):
    T = q.shape[1]
    nk = window // dilation + 1
    steps = jnp.arange(nk) * dilation
    t = jnp.arange(T)[:, None]
    idx = kext.shape[1] - T + t - steps[None, :]
    kg = kext[:, idx]
    vg = vext[:, idx]
    s = jnp.einsum('bthe,btkhe->bthk', q, kg, preferred_element_type=jnp.float32)
    bias = rel_bias[t5_bucket(steps)].astype(jnp.float32).T
    valid = (past_len + t - steps[None, :]) >= 0
    s = jnp.where(valid[None, :, None, :], s + bias[None, None], NEG_INF)
    m = jnp.max(s, axis=-1)
    p = jnp.exp(s - m[..., None])
    l = jnp.sum(p, axis=-1)
    num = jnp.einsum('bthk,btkhe->bthe', p, vg.astype(jnp.float32))
    return num, m, l


def combine_branches(branches):
    M = branches[0][1]
    for _, m, _ in branches[1:]:
        M = jnp.maximum(M, m)
    numer = sum(jnp.exp(m - M)[..., None] * num for num, m, _ in branches)
    denom = sum(jnp.exp(m - M) * l for _, m, l in branches)
    return numer / denom[..., None]


def causal_dwconv(u_ext, w, b):
    out = lax.conv_general_dilated(
        u_ext, w[:, None, :].astype(u_ext.dtype), window_strides=(1,), padding='VALID',
        dimension_numbers=('NWC', 'WIO', 'NWC'), feature_group_count=u_ext.shape[-1])
    return out + b


def expert_mlp(xb, w1, w3, w2):
    return (jax.nn.silu(xb @ w1) * (xb @ w3)) @ w2


def hierarchical_moe(h, w_rg, b_rg, w_re, b_re, w_e1, w_e3, w_e2):
    T, D = h.shape
    hf = h.astype(jnp.float32)
    lg = hf @ w_rg.astype(jnp.float32) + b_rg.astype(jnp.float32)
    pg = jax.nn.softmax(lg, axis=-1)
    grp = jnp.argmax(lg, axis=-1)
    p_grp = jnp.take_along_axis(pg, grp[:, None], axis=-1)
    le = (hf @ w_re.astype(jnp.float32) + b_re.astype(jnp.float32)).reshape(T, N_GROUPS, EXPERTS_PER_GROUP)
    le = jnp.take_along_axis(le, grp[:, None, None], axis=1)[:, 0]
    top_p, top_i = lax.top_k(jax.nn.softmax(le, axis=-1), TOP_K)
    wts = p_grp * top_p / jnp.sum(top_p, axis=-1, keepdims=True)
    eid = grp[:, None] * EXPERTS_PER_GROUP + top_i

    A = T * TOP_K
    e_flat = eid.reshape(A).astype(jnp.int32)
    tok = jnp.repeat(jnp.arange(T, dtype=jnp.int32), TOP_K)
    order = jnp.argsort(e_flat)
    e_s, tok_s, w_s = e_flat[order], tok[order], wts.reshape(A)[order]
    counts = jnp.bincount(e_flat, length=N_EXPERTS)
    starts = jnp.cumsum(counts) - counts
    padded = (counts + EXPERT_BLK - 1) // EXPERT_BLK * EXPERT_BLK
    pends = jnp.cumsum(padded)
    pstarts = pends - padded
    dest = pstarts[e_s] + (jnp.arange(A, dtype=jnp.int32) - starts[e_s])
    NB = -(-A // EXPERT_BLK) + N_EXPERTS
    slot_tok = jnp.full((NB * EXPERT_BLK,), T, dtype=jnp.int32).at[dest].set(tok_s)
    xb = jnp.concatenate([h, jnp.zeros((1, D), h.dtype)], axis=0)[slot_tok].reshape(NB, EXPERT_BLK, D)
    blk_e = jnp.minimum(jnp.searchsorted(pends, jnp.arange(NB) * EXPERT_BLK, side='right'), N_EXPERTS - 1)
    yb = lax.map(lambda a: expert_mlp(a[0], w_e1[a[1]], w_e3[a[1]], w_e2[a[1]]), (xb, blk_e))
    y_assign = yb.reshape(NB * EXPERT_BLK, D)[dest]
    return jnp.zeros((T, D), h.dtype).at[tok_s].add(w_s[:, None].astype(h.dtype) * y_assign)


def decoder_layer(x, c, kbuf, vbuf, conv_buf, is_prompt, rel_bias, w_ada, b_ada, g_n1, w_in,
                  conv_w, conv_b, ln_g, ln_b, w_out, g_n2, w_rg, b_rg, w_re, b_re, w_e1, w_e3, w_e2):
    B, T, D = x.shape
    mod = (jax.nn.silu(c) @ w_ada + b_ada)[:, None, :]
    sh1, sc1, gt1, sh2, sc2, gt2 = jnp.split(mod, 6, axis=-1)
    h = rmsnorm(x, g_n1) * (1 + sc1) + sh1
    proj = h @ w_in
    q, k, v, u_val, u_gate = jnp.split(proj, [D_ATT, 2 * D_ATT, 3 * D_ATT, 3 * D_ATT + D_CONV], axis=-1)
    q = q.reshape(B, T, N_HEADS, HEAD_DIM) * (HEAD_DIM ** -0.5)
    k = k.reshape(B, T, N_HEADS, HEAD_DIM)
    v = v.reshape(B, T, N_HEADS, HEAD_DIM)
    kext = jnp.concatenate([kbuf, k], axis=1)
    vext = jnp.concatenate([vbuf, v], axis=1)
    if is_prompt:
        branches = [banded_dilated_branch(q, k, v, rel_bias, w, d) for w, d in DILATED_CFGS]
    else:
        branches = [gathered_dilated_branch(q, kext, vext, rel_bias, w, d, PAST_LEN) for w, d in DILATED_CFGS]
    att = combine_branches(branches).reshape(B, T, D_ATT).astype(x.dtype)
    new_k = kext[:, -WIN_MAX:]
    new_v = vext[:, -WIN_MAX:]
    glu = u_val * jax.nn.sigmoid(u_gate)
    cext = jnp.concatenate([conv_buf, glu], axis=1)
    conv = jax.nn.silu(layernorm(causal_dwconv(cext, conv_w, conv_b), ln_g, ln_b))
    new_conv = cext[:, -(CONV_W - 1):]
    x = x + gt1 * (jnp.concatenate([att, conv], axis=-1) @ w_out)
    h2 = rmsnorm(x, g_n2) * (1 + sc2) + sh2
    x = x + gt2 * hierarchical_moe(h2.reshape(B * T, D), w_rg, b_rg, w_re, b_re, w_e1, w_e3, w_e2).reshape(B, T, D)
    return x, new_k, new_v, new_conv


def setup_inputs(seed: int = 0) -> dict:
    key = jax.random.key(seed)
    ks = jax.random.split(key, 26)
    nrm = lambda k, s, sc: jax.random.normal(k, s, jnp.float32) * sc
    D = D_MODEL
    return {
        'x_prompt': nrm(ks[0], (BATCH, SEQ, D), 1.0),
        'x_sample': nrm(ks[1], (DEC_BATCH, DEC_SEQ, D), 1.0),
        'cache_k': nrm(ks[2], (DEPTH, DEC_BATCH, WIN_MAX, N_HEADS, HEAD_DIM), 1.0),
        'cache_v': nrm(ks[3], (DEPTH, DEC_BATCH, WIN_MAX, N_HEADS, HEAD_DIM), 1.0),
        'state_conv': nrm(ks[4], (DEPTH, DEC_BATCH, CONV_W - 1, D_CONV), 0.5),
        'c_prompt': nrm(ks[5], (BATCH, D), 1.0),
        'c_sample': nrm(ks[6], (DEC_BATCH, D), 1.0),
        'rel_bias': nrm(ks[7], (NUM_BUCKETS, N_HEADS), 0.1),
        'w_ada': nrm(ks[8], (DEPTH, D, 6 * D), 0.5 * D ** -0.5),
        'b_ada': nrm(ks[9], (DEPTH, 6 * D), 0.01),
        'g_norm1': 1.0 + nrm(ks[10], (DEPTH, D), 0.01),
        'w_in': nrm(ks[11], (DEPTH, D, D_IN), D ** -0.5),
        'conv_w': nrm(ks[12], (DEPTH, CONV_W, D_CONV), CONV_W ** -0.5),
        'conv_b': nrm(ks[13], (DEPTH, D_CONV), 0.01),
        'ln_g': 1.0 + nrm(ks[14], (DEPTH, D_CONV), 0.01),
        'ln_b': nrm(ks[15], (DEPTH, D_CONV), 0.01),
        'w_out': nrm(ks[16], (DEPTH, D_MIX, D), D_MIX ** -0.5),
        'g_norm2': 1.0 + nrm(ks[17], (DEPTH, D), 0.01),
        'w_router_group': nrm(ks[18], (DEPTH, D, N_GROUPS), D ** -0.5),
        'b_router_group': nrm(ks[19], (DEPTH, N_GROUPS), 0.01),
        'w_router_expert': nrm(ks[20], (DEPTH, D, N_EXPERTS), D ** -0.5),
        'b_router_expert': nrm(ks[21], (DEPTH, N_EXPERTS), 0.01),
        'w_expert_gate': nrm(ks[22], (DEPTH, N_EXPERTS, D, D_EXPERT), D ** -0.5),
        'w_expert_up': nrm(ks[23], (DEPTH, N_EXPERTS, D, D_EXPERT), D ** -0.5),
        'w_expert_down': nrm(ks[24], (DEPTH, N_EXPERTS, D_EXPERT, D), D_EXPERT ** -0.5),
        'g_final': 1.0 + nrm(ks[25], (D,), 0.01),
    }


def reference(x_prompt, x_sample, cache_k, cache_v, state_conv, c_prompt, c_sample, rel_bias,
              w_ada, b_ada, g_norm1, w_in, conv_w, conv_b, ln_g, ln_b, w_out, g_norm2,
              w_router_group, b_router_group, w_router_expert, b_router_expert,
              w_expert_gate, w_expert_up, w_expert_down, g_final):
    yp, ys = x_prompt, x_sample
    bp = x_prompt.shape[0]
    zero_kv = jnp.zeros((bp, WIN_MAX, N_HEADS, HEAD_DIM), x_prompt.dtype)
    zero_conv = jnp.zeros((bp, CONV_W - 1, D_CONV), x_prompt.dtype)
    kp_l, vp_l, cp_l, ks_l, vs_l, cs_l = [], [], [], [], [], []
    for l in range(DEPTH):
        w = (rel_bias, w_ada[l], b_ada[l], g_norm1[l], w_in[l], conv_w[l], conv_b[l], ln_g[l], ln_b[l],
             w_out[l], g_norm2[l], w_router_group[l], b_router_group[l], w_router_expert[l],
             b_router_expert[l], w_expert_gate[l], w_expert_up[l], w_expert_down[l])
        yp, kp, vp, cp = decoder_layer(yp, c_prompt, zero_kv, zero_kv, zero_conv, True, *w)
        ys, kS, vS, cS = decoder_layer(ys, c_sample, cache_k[l], cache_v[l], state_conv[l], False, *w)
        kp_l.append(kp); vp_l.append(vp); cp_l.append(cp)
        ks_l.append(kS); vs_l.append(vS); cs_l.append(cS)
    y_prompt = rmsnorm(yp, g_final)
    y_sample = rmsnorm(ys, g_final)
    return (y_prompt, y_sample, jnp.stack(kp_l), jnp.stack(vp_l), jnp.stack(cp_l),
            jnp.stack(ks_l), jnp.stack(vs_l), jnp.stack(cs_l))
```

```python
import functools
import math

import numpy as np
import jax
import jax.numpy as jnp
from jax import lax
from jax.experimental import pallas as pl
from jax.experimental.pallas import tpu as pltpu

F32 = jnp.float32
BF16 = jnp.bfloat16

D_MODEL = 1024
HEAD_DIM = 64
N_HEADS = 8
D_ATT = N_HEADS * HEAD_DIM
D_CONV = D_MODEL - D_ATT
DILATED_CFGS = ((128, 1), (512, 4), (2048, 16))
WIN_MAX = max(w for w, _ in DILATED_CFGS)
ATT_BLK = 128
CONV_W = 31
NUM_BUCKETS = 32
MAX_DISTANCE = WIN_MAX
N_GROUPS = 4
EXPERTS_PER_GROUP = 8
N_EXPERTS = N_GROUPS * EXPERTS_PER_GROUP
D_EXPERT = 512
EPS = 1e-6
NEG_INF = -1e30
PAST_LEN = 8192

LANES = 128
ROW_TILE = 8
CONV_PAD = 32
ROUTE_LANES = 128
GROUP_LANE0 = N_EXPERTS
MOE_BLK = 256
VMEM_LIMIT = 56 * 1024 * 1024


def _cparams(sem):
    return pltpu.CompilerParams(dimension_semantics=sem, vmem_limit_bytes=VMEM_LIMIT)


def _sigmoid(x):
    return 1.0 / (1.0 + jnp.exp(-x))


def _silu(x):
    return x * _sigmoid(x)


def _rms_scale(x):
    return x * lax.rsqrt(jnp.mean(x * x, axis=-1, keepdims=True) + EPS)


def _dot(a, b):
    return jnp.dot(a, b, preferred_element_type=F32)


def _ada_body(c_ref, w_ref, b_ref, o_ref):
    s = _silu(c_ref[...]).astype(BF16)
    o_ref[...] = _dot(s, w_ref[...].astype(BF16)) + b_ref[...]


def ada_modulation(c, w_ada, b_ada):
    rows, d = c.shape
    n = w_ada.shape[1]
    tn = 1024
    return pl.pallas_call(
        _ada_body,
        grid=(n // tn,),
        in_specs=[pl.BlockSpec((rows, d), lambda j: (0, 0)),
                  pl.BlockSpec((d, tn), lambda j: (0, j)),
                  pl.BlockSpec((1, tn), lambda j: (0, j))],
        out_specs=pl.BlockSpec((rows, tn), lambda j: (0, j)),
        out_shape=jax.ShapeDtypeStruct((rows, n), F32),
        compiler_params=_cparams(("arbitrary",)),
        name="ada_modulation",
    )(c, w_ada, b_ada.reshape(1, n))


def _conv_ln_swish(cbuf, tm, cw_ref, cb_ref, lg_ref, lb_ref, out_ref):
    chunk = 32
    base0 = CONV_PAD - (CONV_W - 1)
    for c in range(tm // chunk):
        acc = jnp.broadcast_to(cb_ref[...], (chunk, D_CONV))
        for j in range(CONV_W):
            r0 = base0 + j + c * chunk
            acc = acc + cw_ref[j:j + 1, :] * cbuf[r0:r0 + chunk, :]
        mu = jnp.mean(acc, axis=-1, keepdims=True)
        xc = acc - mu
        var = jnp.mean(xc * xc, axis=-1, keepdims=True)
        y = xc * lax.rsqrt(var + EPS) * lg_ref[...] + lb_ref[...]
        out_ref[0, c * chunk:(c + 1) * chunk, :] = _silu(y).astype(out_ref.dtype)


def _inproj_body(x_ref, sc_ref, sh_ref, g_ref, w_ref, cw_ref, cb_ref, lg_ref, lb_ref,
                 q_ref, k_ref, v_ref, kc_ref, vc_ref, conv_ref, tail_ref, cbuf, *, first_cache_tile):
    i = pl.program_id(1)
    tm = x_ref.shape[1]

    @pl.when(i == 0)
    def _():
        cbuf[0:CONV_PAD, :] = jnp.zeros((CONV_PAD, D_CONV), F32)

    x = x_ref[0]
    h = (_rms_scale(x) * g_ref[...] * (1.0 + sc_ref[0]) + sh_ref[0]).astype(BF16)

    def col(j):
        return _dot(h, w_ref[:, j * D_ATT:(j + 1) * D_ATT])

    q_ref[0] = (col(0) * (HEAD_DIM ** -0.5)).astype(BF16)
    k = col(1)
    k_ref[0] = k.astype(BF16)
    v = col(2)
    v_ref[0] = v.astype(BF16)

    @pl.when(i >= first_cache_tile)
    def _():
        kc_ref[0] = k.T
        vc_ref[0] = v.T

    glu = col(3) * _sigmoid(col(4))
    cbuf[CONV_PAD:CONV_PAD + tm, :] = glu
    tail_ref[0] = glu[tm - CONV_PAD:, :]
    _conv_ln_swish(cbuf, tm, cw_ref, cb_ref, lg_ref, lb_ref, conv_ref)
    cbuf[0:CONV_PAD, :] = cbuf[tm:tm + CONV_PAD, :]


def prompt_inproj(x, sc1, sh1, g1, w_in_bf16, conv_w, conv_b, ln_g, ln_b, tm=512):
    b, s, d = x.shape
    nt = s // tm
    first_cache_tile = (s - WIN_MAX) // tm
    row = lambda v: v.reshape(1, -1)
    tok = lambda bi, i: (bi, i, 0)
    per_b = lambda bi, i: (bi, 0, 0)
    cst = lambda bi, i: (0, 0)
    cache = lambda bi, i: (bi, 0, jnp.maximum(i - first_cache_tile, 0))
    return pl.pallas_call(
        functools.partial(_inproj_body, first_cache_tile=first_cache_tile),
        grid=(b, nt),
        in_specs=[pl.BlockSpec((1, tm, d), tok),
                  pl.BlockSpec((1, 1, d), per_b),
                  pl.BlockSpec((1, 1, d), per_b),
                  pl.BlockSpec((1, d), cst),
                  pl.BlockSpec(w_in_bf16.shape, cst),
                  pl.BlockSpec(conv_w.shape, cst),
                  pl.BlockSpec((1, D_CONV), cst),
                  pl.BlockSpec((1, D_CONV), cst),
                  pl.BlockSpec((1, D_CONV), cst)],
        out_specs=[pl.BlockSpec((1, tm, D_ATT), tok),
                   pl.BlockSpec((1, tm, D_ATT), tok),
                   pl.BlockSpec((1, tm, D_ATT), tok),
                   pl.BlockSpec((1, D_ATT, tm), cache),
                   pl.BlockSpec((1, D_ATT, tm), cache),
                   pl.BlockSpec((1, tm, D_CONV), tok),
                   pl.BlockSpec((1, CONV_PAD, D_CONV), per_b)],
        out_shape=[jax.ShapeDtypeStruct((b, s, D_ATT), BF16),
                   jax.ShapeDtypeStruct((b, s, D_ATT), BF16),
                   jax.ShapeDtypeStruct((b, s, D_ATT), BF16),
                   jax.ShapeDtypeStruct((b, D_ATT, WIN_MAX), F32),
                   jax.ShapeDtypeStruct((b, D_ATT, WIN_MAX), F32),
                   jax.ShapeDtypeStruct((b, s, D_CONV), BF16),
                   jax.ShapeDtypeStruct((b, CONV_PAD, D_CONV), F32)],
        scratch_shapes=[pltpu.VMEM((tm + CONV_PAD, D_CONV), F32)],
        compiler_params=_cparams(("arbitrary", "arbitrary")),
        name="prompt_inproj_conv",
    )(x, sc1, sh1, row(g1), w_in_bf16, conv_w, row(conv_b), row(ln_g), row(ln_b))


def t5_bucket_np(dist):
    n = np.maximum(dist, 0)
    max_exact = NUM_BUCKETS // 2
    nf = np.maximum(n, 1).astype(np.float32)
    large = max_exact + (np.log(nf / np.float32(max_exact)) / np.float32(math.log(MAX_DISTANCE / max_exact))
                         * np.float32(NUM_BUCKETS - max_exact)).astype(np.int32)
    large = np.minimum(large, NUM_BUCKETS - 1)
    return np.where(n < max_exact, n, large)


def _band_rel():
    i = np.arange(ATT_BLK)[:, None]
    j = np.arange(2 * ATT_BLK)[None, :]
    return i + ATT_BLK - j


def _attn_body(q_ref, kc_ref, kp_ref, vc_ref, vp_ref, bias_ref, o_ref, lse_ref, *, wsub):
    i = pl.program_id(2)
    nq = q_ref.shape[1] // ATT_BLK
    qi = lax.broadcasted_iota(jnp.int32, (ATT_BLK, 2 * ATT_BLK), 0)
    kj = lax.broadcasted_iota(jnp.int32, (ATT_BLK, 2 * ATT_BLK), 1)
    rel = qi + ATT_BLK - kj
    band = (rel >= 0) & (rel <= wsub)
    band_first = band & ((kj >= ATT_BLK) | (i > 0))
    for sub in range(nq):
        r0 = sub * ATT_BLK
        valid = band_first if sub == 0 else band
        for h in range(N_HEADS):
            c0 = h * HEAD_DIM
            q = q_ref[0, r0:r0 + ATT_BLK, c0:c0 + HEAD_DIM]
            if sub == 0:
                k = jnp.concatenate([kp_ref[0, :, c0:c0 + HEAD_DIM], kc_ref[0, 0:ATT_BLK, c0:c0 + HEAD_DIM]], axis=0)
                v = jnp.concatenate([vp_ref[0, :, c0:c0 + HEAD_DIM], vc_ref[0, 0:ATT_BLK, c0:c0 + HEAD_DIM]], axis=0)
            else:
                k = kc_ref[0, r0 - ATT_BLK:r0 + ATT_BLK, c0:c0 + HEAD_DIM]
                v = vc_ref[0, r0 - ATT_BLK:r0 + ATT_BLK, c0:c0 + HEAD_DIM]
            s = lax.dot_general(q, k, (((1,), (1,)), ((), ())), preferred_element_type=F32)
            s = jnp.where(valid, s + bias_ref[h], NEG_INF)
            m = jnp.max(s, axis=-1, keepdims=True)
            p = jnp.exp(s - m)
            l = jnp.sum(p, axis=-1, keepdims=True)
            num = _dot(p.astype(BF16), v)
            o_ref[0, r0:r0 + ATT_BLK, c0:c0 + HEAD_DIM] = (num / l).astype(o_ref.dtype)
            lse_ref[0, r0:r0 + ATT_BLK, c0:c0 + HEAD_DIM] = jnp.broadcast_to(m + jnp.log(l), (ATT_BLK, HEAD_DIM))


def attention_branch(q, k, v, bias, window, dilation, qblocks=4):
    b, s, c = q.shape
    n = s // dilation
    wsub = window // dilation
    tq = ATT_BLK * qblocks
    view = lambda a: a.reshape(b, n, dilation * c)
    cur = lambda bi, r, i: (bi, i, r)
    prev = lambda bi, r, i: (bi, jnp.maximum(i * qblocks - 1, 0), r)
    o, lse = pl.pallas_call(
        functools.partial(_attn_body, wsub=wsub),
        grid=(b, dilation, n // tq),
        in_specs=[pl.BlockSpec((1, tq, c), cur),
                  pl.BlockSpec((1, tq, c), cur),
                  pl.BlockSpec((1, ATT_BLK, c), prev),
                  pl.BlockSpec((1, tq, c), cur),
                  pl.BlockSpec((1, ATT_BLK, c), prev),
                  pl.BlockSpec(bias.shape, lambda bi, r, i: (0, 0, 0))],
        out_specs=[pl.BlockSpec((1, tq, c), cur),
                   pl.BlockSpec((1, tq, c), cur)],
        out_shape=[jax.ShapeDtypeStruct((b, n, dilation * c), BF16),
                   jax.ShapeDtypeStruct((b, n, dilation * c), F32)],
        compiler_params=_cparams(("arbitrary", "arbitrary", "arbitrary")),
        name=f"attn_branch_d{dilation}",
    )(view(q), view(k), view(k), view(v), view(v), bias)
    return o.reshape(b, s, c), lse.reshape(b, s, c)


def _split_bf16(a):
    hi = a.astype(BF16)
    lo = (a - hi.astype(F32)).astype(BF16)
    return hi, lo


def _router_logits(h2, wr_hi_ref, wr_lo_ref, br_ref):
    hi, lo = _split_bf16(h2)
    return _dot(hi, wr_hi_ref[...]) + (_dot(hi, wr_lo_ref[...]) + _dot(lo, wr_hi_ref[...])) + br_ref[...]


def _route(logits):
    rows = logits.shape[0]
    lane = lax.broadcasted_iota(jnp.int32, (rows, ROUTE_LANES), 1)
    lanef = lane.astype(F32)
    big = float(ROUTE_LANES)
    is_g = (lane >= GROUP_LANE0) & (lane < GROUP_LANE0 + N_GROUPS)
    glog = jnp.where(is_g, logits, -jnp.inf)
    gmax = jnp.max(glog, axis=-1, keepdims=True)
    gsum = jnp.sum(jnp.exp(glog - gmax), axis=-1, keepdims=True)
    p_grp = 1.0 / gsum
    glane = jnp.min(jnp.where(glog == gmax, lanef, big), axis=-1, keepdims=True)
    grp = glane - float(GROUP_LANE0)
    in_grp = jnp.floor(lanef * (1.0 / EXPERTS_PER_GROUP)) == grp
    elog = jnp.where(in_grp, logits, -jnp.inf)
    emax = jnp.max(elog, axis=-1, keepdims=True)
    eexp = jnp.exp(elog - emax)
    pe = eexp / jnp.sum(eexp, axis=-1, keepdims=True)
    pe = jnp.where(in_grp, pe, -1.0)
    p1 = jnp.max(pe, axis=-1, keepdims=True)
    e1 = jnp.min(jnp.where(pe == p1, lanef, big), axis=-1, keepdims=True)
    pe2 = jnp.where(lanef == e1, -1.0, pe)
    p2 = jnp.max(pe2, axis=-1, keepdims=True)
    e2 = jnp.min(jnp.where(pe2 == p2, lanef, big), axis=-1, keepdims=True)
    psum = p1 + p2
    return e1, e2, p_grp * p1 / psum, p_grp * p2 / psum


def _mix_branches(o_refs, lse_refs):
    lses = [r[0] for r in lse_refs]
    mx = jnp.maximum(jnp.maximum(lses[0], lses[1]), lses[2])
    ws = [jnp.exp(l - mx) for l in lses]
    numer = sum(w * o[0].astype(F32) for w, o in zip(ws, o_refs))
    return numer / (ws[0] + ws[1] + ws[2])


def _outproj_body(o1, o4, o16, l1, l4, l16, conv_ref, x_ref, gt_ref, sc_ref, sh_ref, g_ref,
                  wo_ref, wrh_ref, wrl_ref, br_ref,
                  x1_ref, h2_ref, route_ref, cnt_ref, carry):
    first = (pl.program_id(0) == 0) & (pl.program_id(1) == 0)

    @pl.when(first)
    def _():
        carry[...] = jnp.zeros_like(carry)

    tm = x_ref.shape[1]
    att = _mix_branches((o1, o4, o16), (l1, l4, l16)).astype(BF16)
    mix = _dot(att, wo_ref[0:D_ATT, :]) + _dot(conv_ref[0], wo_ref[D_ATT:, :])
    x1 = x_ref[0] + gt_ref[0] * mix
    x1_ref[0] = x1
    h2 = _rms_scale(x1) * g_ref[...] * (1.0 + sc_ref[0]) + sh_ref[0]
    h2_ref[0] = h2.astype(BF16)
    e1, e2, w1, w2 = _route(_router_logits(h2, wrh_ref, wrl_ref, br_ref))

    lane = lax.broadcasted_iota(jnp.int32, (tm, ROUTE_LANES), 1).astype(F32)
    oh1 = lane == e1
    oh2 = lane == e2
    onehot = jnp.where(oh1 | oh2, 1.0, 0.0)
    ri = lax.broadcasted_iota(jnp.int32, (tm, tm), 0)
    ci = lax.broadcasted_iota(jnp.int32, (tm, tm), 1)
    tri = jnp.where(ci < ri, 1.0, 0.0).astype(BF16)
    before = _dot(tri, onehot.astype(BF16)) + carry[...]
    r1 = jnp.sum(jnp.where(oh1, before, 0.0), axis=-1, keepdims=True)
    r2 = jnp.sum(jnp.where(oh2, before, 0.0), axis=-1, keepdims=True)
    carry[...] = carry[...] + jnp.sum(onehot, axis=0, keepdims=True)
    cnt_ref[...] = carry[...]
    rec = jnp.where(lane == 0.0, e1, 0.0)
    for idx, val in ((1, e2), (2, w1), (3, w2), (4, r1), (5, r2)):
        rec = jnp.where(lane == float(idx), val, rec)
    route_ref[0] = rec


def prompt_outproj(o_list, lse_list, conv_act, x, gt1, sc2, sh2, g2, w_out_bf16, wr_hi, wr_lo, br, tm=512):
    b, s, d = x.shape
    tok = lambda bi, i: (bi, i, 0)
    per_b = lambda bi, i: (bi, 0, 0)
    cst = lambda bi, i: (0, 0)
    half = pl.BlockSpec((1, tm, D_ATT), tok)
    return pl.pallas_call(
        _outproj_body,
        grid=(b, s // tm),
        in_specs=[half] * 7 + [pl.BlockSpec((1, tm, d), tok),
                               pl.BlockSpec((1, 1, d), per_b),
                               pl.BlockSpec((1, 1, d), per_b),
                               pl.BlockSpec((1, 1, d), per_b),
                               pl.BlockSpec((1, d), cst),
                               pl.BlockSpec(w_out_bf16.shape, cst),
                               pl.BlockSpec(wr_hi.shape, cst),
                               pl.BlockSpec(wr_lo.shape, cst),
                               pl.BlockSpec((1, ROUTE_LANES), cst)],
        out_specs=[pl.BlockSpec((1, tm, d), tok),
                   pl.BlockSpec((1, tm, d), tok),
                   pl.BlockSpec((1, tm, ROUTE_LANES), tok),
                   pl.BlockSpec((1, ROUTE_LANES), cst)],
        out_shape=[jax.ShapeDtypeStruct((b, s, d), F32),
                   jax.ShapeDtypeStruct((b, s, d), BF16),
                   jax.ShapeDtypeStruct((b, s, ROUTE_LANES), F32),
                   jax.ShapeDtypeStruct((1, ROUTE_LANES), F32)],
        scratch_shapes=[pltpu.VMEM((1, ROUTE_LANES), F32)],
        compiler_params=_cparams(("arbitrary", "arbitrary")),
        name="prompt_outproj_router",
    )(*o_list, *lse_list, conv_act, x, gt1, sc2, sh2, g2.reshape(1, d), w_out_bf16, wr_hi, wr_lo, br)


def router_weights(w_rg, b_rg, w_re, b_re):
    d = w_rg.shape[0]
    w = jnp.zeros((d, ROUTE_LANES), F32)
    w = w.at[:, 0:N_EXPERTS].set(w_re.astype(F32)).at[:, GROUP_LANE0:GROUP_LANE0 + N_GROUPS].set(w_rg.astype(F32))
    bias = jnp.zeros((1, ROUTE_LANES), F32)
    bias = bias.at[0, 0:N_EXPERTS].set(b_re.astype(F32)).at[0, GROUP_LANE0:GROUP_LANE0 + N_GROUPS].set(b_rg.astype(F32))
    hi = w.astype(BF16)
    lo = (w - hi.astype(F32)).astype(BF16)
    return hi, lo, bias


def _rows_from_tiles(tile_ref, n):
    return jnp.concatenate([tile_ref[pl.ds(c, n, stride=ROW_TILE), :] for c in range(ROW_TILE)], axis=-1)


def _rows_to_tiles(tile_ref, rows):
    n = rows.shape[0]
    for c in range(ROW_TILE):
        tile_ref[pl.ds(c, n, stride=ROW_TILE), :] = rows[:, c * LANES:(c + 1) * LANES]


def _dispatch_body(dest_ref, pend_ref, h2_ref, xs_hbm, tiles, zbuf, sem, zsem, *, chunk):
    step = pl.program_id(0)

    @pl.when(step == 0)
    def _():
        zbuf[...] = jnp.zeros_like(zbuf)

        def zcopy(e):
            start = pl.multiple_of(jnp.maximum(pend_ref[e + 1] - MOE_BLK * ROW_TILE, 0), ROW_TILE)
            return pltpu.make_async_copy(zbuf, xs_hbm.at[pl.ds(start, MOE_BLK * ROW_TILE)], zsem)

        def zstart(e, c):
            @pl.when(pend_ref[e + 1] > pend_ref[e])
            def _():
                zcopy(e).start()
            return c

        def zwait(e, c):
            @pl.when(pend_ref[e + 1] > pend_ref[e])
            def _():
                zcopy(e).wait()
            return c

        lax.fori_loop(0, N_EXPERTS, zstart, 0)
        lax.fori_loop(0, N_EXPERTS, zwait, 0)

        blk_rows = MOE_BLK * ROW_TILE
        nblk = xs_hbm.shape[0] // blk_rows

        def tcopy(i):
            return pltpu.make_async_copy(zbuf, xs_hbm.at[pl.ds(pl.multiple_of(i * blk_rows, blk_rows), blk_rows)],
                                         zsem)

        def tstart(i, c):
            @pl.when(i * blk_rows >= pend_ref[N_EXPERTS])
            def _():
                tcopy(i).start()
            return c

        def twait(i, c):
            @pl.when(i * blk_rows >= pend_ref[N_EXPERTS])
            def _():
                tcopy(i).wait()
            return c

        lax.fori_loop(0, nblk, tstart, 0)
        lax.fori_loop(0, nblk, twait, 0)

    _rows_to_tiles(tiles, h2_ref[...].astype(F32))

    def copy(a):
        src = pl.multiple_of((a // 2) * ROW_TILE, ROW_TILE)
        dst = pl.multiple_of(dest_ref[0, 0, a], ROW_TILE)
        return pltpu.make_async_copy(tiles.at[pl.ds(src, ROW_TILE)], xs_hbm.at[pl.ds(dst, ROW_TILE)], sem)

    def start(a, c):
        copy(a).start()
        return c

    def wait(a, c):
        copy(a).wait()
        return c

    lax.fori_loop(0, 2 * chunk, start, 0, unroll=8)
    lax.fori_loop(0, 2 * chunk, wait, 0, unroll=8)


def moe_dispatch(h2_flat, dest_rt, pends_rt, nslots, chunk=512):
    t, d = h2_flat.shape
    assert d == ROW_TILE * LANES
    nsteps = t // chunk
    dest3 = dest_rt.reshape(nsteps, 1, 2 * chunk)
    return pl.pallas_call(
        functools.partial(_dispatch_body, chunk=chunk),
        grid=(nsteps,),
        in_specs=[pl.BlockSpec((1, 1, 2 * chunk), lambda i: (i, 0, 0), memory_space=pltpu.SMEM),
                  pl.BlockSpec(memory_space=pltpu.SMEM),
                  pl.BlockSpec((chunk, d), lambda i: (i, 0))],
        out_specs=pl.BlockSpec(memory_space=pl.ANY),
        out_shape=jax.ShapeDtypeStruct((nslots * ROW_TILE, LANES), F32),
        scratch_shapes=[pltpu.VMEM((chunk * ROW_TILE, LANES), F32),
                        pltpu.VMEM((MOE_BLK * ROW_TILE, LANES), F32),
                        pltpu.SemaphoreType.DMA(()),
                        pltpu.SemaphoreType.DMA(())],
        compiler_params=_cparams(("arbitrary",)),
        name="moe_dispatch",
    )(dest3, pends_rt, h2_flat)


def _experts_body(blk_e_ref, nused_ref, xs_ref, w1_ref, w3_ref, w2_ref, y_ref, w1b, w3b, w2b):
    i = pl.program_id(0)
    new_expert = (i == 0) | (blk_e_ref[i] != blk_e_ref[jnp.maximum(i - 1, 0)])

    @pl.when((i < nused_ref[0]) & new_expert)
    def _():
        w1b[...] = w1_ref[0].astype(BF16)
        w3b[...] = w3_ref[0].astype(BF16)
        w2b[...] = w2_ref[0].astype(BF16)

    @pl.when(i < nused_ref[0])
    def _():
        xb = _rows_from_tiles(xs_ref, MOE_BLK).astype(BF16)
        a = (_silu(_dot(xb, w1b[...])) * _dot(xb, w3b[...])).astype(BF16)
        _rows_to_tiles(y_ref, _dot(a, w2b[...]))

    @pl.when(i >= nused_ref[0])
    def _():
        y_ref[...] = jnp.zeros_like(y_ref)


def moe_experts(xs, blk_e, nused, w_e1, w_e3, w_e2):
    nb = xs.shape[0] // (MOE_BLK * ROW_TILE)
    _, d, de = w_e1.shape
    blk = lambda i, be, nu: (jnp.minimum(i, nu[0] - 1), 0)
    wsel = lambda i, be, nu: (be[i], 0, 0)
    grid_spec = pltpu.PrefetchScalarGridSpec(
        num_scalar_prefetch=2,
        grid=(nb,),
        in_specs=[pl.BlockSpec((MOE_BLK * ROW_TILE, LANES), blk),
                  pl.BlockSpec((1, d, de), wsel),
                  pl.BlockSpec((1, d, de), wsel),
                  pl.BlockSpec((1, de, d), wsel)],
        out_specs=pl.BlockSpec((MOE_BLK * ROW_TILE, LANES), lambda i, be, nu: (i, 0)),
        scratch_shapes=[pltpu.VMEM((d, de), BF16), pltpu.VMEM((d, de), BF16), pltpu.VMEM((de, d), BF16)],
    )
    return pl.pallas_call(
        _experts_body,
        grid_spec=grid_spec,
        out_shape=jax.ShapeDtypeStruct(xs.shape, F32),
        compiler_params=_cparams(("arbitrary",)),
        name="moe_experts",
    )(blk_e, nused, xs, w_e1, w_e3, w_e2)


def _combine_body(dest_ref, yb_hbm, x1_ref, route_ref, gt_ref, gf_ref, out_ref, ya0, ya1, sem, *, tm):
    def copy(a, buf):
        src = pl.multiple_of(dest_ref[0, 0, a], ROW_TILE)
        dst = pl.multiple_of((a // 2) * ROW_TILE, ROW_TILE)
        return pltpu.make_async_copy(yb_hbm.at[pl.ds(src, ROW_TILE)], buf.at[pl.ds(dst, ROW_TILE)], sem)

    def start(t, c):
        copy(2 * t, ya0).start()
        copy(2 * t + 1, ya1).start()
        return c

    def wait(t, c):
        copy(2 * t, ya0).wait()
        copy(2 * t + 1, ya1).wait()
        return c

    lax.fori_loop(0, tm, start, 0, unroll=8)
    lax.fori_loop(0, tm, wait, 0, unroll=8)
    rec = route_ref[0]
    moe = rec[:, 2:3] * _rows_from_tiles(ya0, tm) + rec[:, 3:4] * _rows_from_tiles(ya1, tm)
    y = x1_ref[0] + gt_ref[0] * moe
    out_ref[0] = _rms_scale(y) * gf_ref[...]


def moe_combine(yb, dest_rt, x1, route, gt2, g_final, tm=256):
    b, s, d = x1.shape
    nt = s // tm
    dest3 = dest_rt.reshape(b * nt, 1, 2 * tm)
    tok = lambda bi, i: (bi, i, 0)
    return pl.pallas_call(
        functools.partial(_combine_body, tm=tm),
        grid=(b, nt),
        in_specs=[pl.BlockSpec((1, 1, 2 * tm), lambda bi, i: (bi * nt + i, 0, 0), memory_space=pltpu.SMEM),
                  pl.BlockSpec(memory_space=pl.ANY),
                  pl.BlockSpec((1, tm, d), tok),
                  pl.BlockSpec((1, tm, ROUTE_LANES), tok),
                  pl.BlockSpec((1, 1, d), lambda bi, i: (bi, 0, 0)),
                  pl.BlockSpec((1, d), lambda bi, i: (0, 0))],
        out_specs=pl.BlockSpec((1, tm, d), tok),
        out_shape=jax.ShapeDtypeStruct((b, s, d), F32),
        scratch_shapes=[pltpu.VMEM((tm * ROW_TILE, LANES), F32), pltpu.VMEM((tm * ROW_TILE, LANES), F32),
                        pltpu.SemaphoreType.DMA(())],
        compiler_params=_cparams(("arbitrary", "arbitrary")),
        name="moe_combine_final",
    )(dest3, yb, x1, route, gt2, g_final.reshape(1, d))


def slot_plan(route, counts, nblocks):
    cnt = counts[0, :N_EXPERTS].astype(jnp.int32)
    padded = (cnt + MOE_BLK - 1) // MOE_BLK * MOE_BLK
    pends = jnp.cumsum(padded)
    pstarts = pends - padded
    eid = route[..., 0:2].astype(jnp.int32)
    rank = route[..., 4:6].astype(jnp.int32)
    dest = (pstarts[eid] + rank).reshape(-1)
    blk_e = jnp.minimum(jnp.searchsorted(pends, jnp.arange(nblocks, dtype=jnp.int32) * MOE_BLK, side='right'),
                        N_EXPERTS - 1).astype(jnp.int32)
    nused = (pends[-1:] // MOE_BLK).astype(jnp.int32)
    pend_ext = jnp.concatenate([jnp.zeros((1,), jnp.int32), pends.astype(jnp.int32)])
    return (dest * ROW_TILE).astype(jnp.int32), blk_e, nused, (pend_ext * ROW_TILE).astype(jnp.int32)


def _sample_inproj_body(x_ref, sc_ref, sh_ref, g_ref, w_ref, q_ref, k_ref, v_ref, glu_ref):
    h = (_rms_scale(x_ref[...]) * g_ref[...] * (1.0 + sc_ref[...]) + sh_ref[...]).astype(BF16)

    def col(j):
        return _dot(h, w_ref[:, j * D_ATT:(j + 1) * D_ATT])

    q_ref[...] = col(0) * (HEAD_DIM ** -0.5)
    k_ref[...] = col(1)
    v_ref[...] = col(2)
    glu_ref[...] = col(3) * _sigmoid(col(4))


def sample_inproj(x, sc1, sh1, g1, w_in_bf16):
    n, d = x.shape
    out = jax.ShapeDtypeStruct((n, D_ATT), F32)
    return pl.pallas_call(
        _sample_inproj_body,
        out_shape=[out, out, out, out],
        compiler_params=pltpu.CompilerParams(vmem_limit_bytes=VMEM_LIMIT),
        name="sample_inproj",
    )(x, sc1, sh1, g1.reshape(1, d), w_in_bf16)


def _sample_cache_body(qt_ref, knt_ref, vnt_ref, k_ref, v_ref, bias_ref, valid_ref, bias0_ref, valid0_ref,
                       ko_ref, vo_ref, att_ref, s_s, p_s, n_s):
    b = pl.program_id(0)
    win = k_ref.shape[2]
    sel = lax.broadcasted_iota(jnp.int32, qt_ref.shape, 1) == b

    def column(ref):
        return jnp.sum(jnp.where(sel, ref[...], 0.0), axis=1, keepdims=True)

    q, kn, vn = column(qt_ref), column(knt_ref), column(vnt_ref)
    kt = k_ref[0]
    vt = v_ref[0]
    for h in range(N_HEADS):
        rows = slice(h * HEAD_DIM, (h + 1) * HEAD_DIM)
        s_s[h:h + 1, :] = jnp.sum(kt[rows, :] * q[rows, :], axis=0, keepdims=True)
        n_s[h:h + 1, :] = jnp.broadcast_to(jnp.sum(kn[rows, :] * q[rows, :], axis=0, keepdims=True), (1, LANES))
    s_all = s_s[...]
    s_new = n_s[...][:, 0:1] + bias0_ref[:, 0:1]
    parts = []
    for br in range(len(DILATED_CFGS)):
        s = jnp.where(valid_ref[br] > 0.0, s_all + bias_ref[br], NEG_INF)
        s0 = jnp.where(valid0_ref[br][:, 0:1] > 0.0, s_new, NEG_INF)
        m = jnp.maximum(jnp.max(s, axis=-1, keepdims=True), s0)
        p = jnp.exp(s - m)
        p0 = jnp.exp(s0 - m)
        parts.append((p, p0, m, jnp.sum(p, axis=-1, keepdims=True) + p0))
    mx = jnp.maximum(jnp.maximum(parts[0][2], parts[1][2]), parts[2][2])
    ws = [jnp.exp(m - mx) for _, _, m, _ in parts]
    inv = 1.0 / sum(w * l for w, (_, _, _, l) in zip(ws, parts))
    p_s[...] = sum(w * p for w, (p, _, _, _) in zip(ws, parts)) * inv
    p_new = sum(w * p0 for w, (_, p0, _, _) in zip(ws, parts)) * inv
    cols = []
    for h in range(N_HEADS):
        rows = slice(h * HEAD_DIM, (h + 1) * HEAD_DIM)
        cols.append(jnp.sum(vt[rows, :] * p_s[h:h + 1, :], axis=1, keepdims=True) + p_new[h:h + 1, :] * vn[rows, :])
    att = jnp.concatenate(cols, axis=0)

    @pl.when(b == 0)
    def _():
        att_ref[...] = jnp.zeros_like(att_ref)

    att_ref[...] = jnp.where(sel, att, att_ref[...])
    last = lax.broadcasted_iota(jnp.int32, kt.shape, 1) == win - 1
    ko_ref[0] = jnp.where(last, kn, pltpu.roll(kt, win - 1, 1))
    vo_ref[0] = jnp.where(last, vn, pltpu.roll(vt, win - 1, 1))


def sample_cache_attention(q, k_new, v_new, cache_kt, cache_vt, bias, valid, bias0, valid0):
    n, c, win = cache_kt.shape
    full = lambda a: pl.BlockSpec(a.shape, lambda i: (0,) * a.ndim)
    seq = pl.BlockSpec((1, c, win), lambda i: (i, 0, 0))
    qt, knt, vnt = q.T, k_new.T, v_new.T
    ko, vo, att_t = pl.pallas_call(
        _sample_cache_body,
        grid=(n,),
        in_specs=[full(qt), full(knt), full(vnt), seq, seq, full(bias), full(valid), full(bias0), full(valid0)],
        out_specs=[seq, seq, pl.BlockSpec((c, n), lambda i: (0, 0))],
        out_shape=[jax.ShapeDtypeStruct(cache_kt.shape, F32), jax.ShapeDtypeStruct(cache_vt.shape, F32),
                   jax.ShapeDtypeStruct((c, n), F32)],
        scratch_shapes=[pltpu.VMEM((N_HEADS, win), F32), pltpu.VMEM((N_HEADS, win), F32),
                        pltpu.VMEM((N_HEADS, LANES), F32)],
        compiler_params=_cparams(("arbitrary",)),
        name="sample_cache_attention",
    )(qt, knt, vnt, cache_kt, cache_vt, bias, valid, bias0, valid0)
    return ko, vo, att_t.T


def _sample_tail_body(att_ref, glu_ref, st_ref, x_ref, gt1_ref, sc2_ref, sh2_ref, gt2_ref,
                      cw_ref, cb_ref, lg_ref, lb_ref, wo_ref, g2_ref, wrh_ref, wrl_ref, br_ref, gf_ref,
                      w1_ref, w3_ref, w2_ref, y_ref, x1_s, h2_s, gate_s, acc_s):
    e = pl.program_id(0)
    n = x_ref.shape[0]

    @pl.when(e == 0)
    def _():
        nhist = CONV_W - 1
        conv = cw_ref[nhist:CONV_W, :] * glu_ref[...] + cb_ref[...]
        for j in range(nhist):
            conv = conv + cw_ref[j:j + 1, :] * st_ref[j]
        mu = jnp.mean(conv, axis=-1, keepdims=True)
        xc = conv - mu
        var = jnp.mean(xc * xc, axis=-1, keepdims=True)
        act = _silu(xc * lax.rsqrt(var + EPS) * lg_ref[...] + lb_ref[...]).astype(BF16)
        mix = _dot(att_ref[...].astype(BF16), wo_ref[0:D_ATT, :]) + _dot(act, wo_ref[D_ATT:, :])
        x1 = x_ref[...] + gt1_ref[...] * mix
        x1_s[...] = x1
        h2 = _rms_scale(x1) * g2_ref[...] * (1.0 + sc2_ref[...]) + sh2_ref[...]
        h2_s[...] = h2.astype(BF16)
        e1, e2, w1, w2 = _route(_router_logits(h2, wrh_ref, wrl_ref, br_ref))
        lane = lax.broadcasted_iota(jnp.int32, (n, ROUTE_LANES), 1).astype(F32)
        gate_s[...] = jnp.where(lane == e1, w1, 0.0) + jnp.where(lane == e2, w2, 0.0)
        acc_s[...] = jnp.zeros_like(acc_s)

    hb = h2_s[...]
    a = (_silu(_dot(hb, w1_ref[0].astype(BF16))) * _dot(hb, w3_ref[0].astype(BF16))).astype(BF16)
    ye = _dot(a, w2_ref[0].astype(BF16))
    lane = lax.broadcasted_iota(jnp.int32, (n, ROUTE_LANES), 1)
    g = jnp.sum(jnp.where(lane == e, gate_s[...], 0.0), axis=-1, keepdims=True)
    acc_s[...] = acc_s[...] + g * ye

    @pl.when(e == pl.num_programs(0) - 1)
    def _():
        y = x1_s[...] + gt2_ref[...] * acc_s[...]
        y_ref[...] = _rms_scale(y) * gf_ref[...]


def sample_tail(att, glu, state_conv, x, gt1, sc2, sh2, gt2, conv_w, conv_b, ln_g, ln_b, w_out_bf16, g2,
                wr_hi, wr_lo, br, g_final, w_e1, w_e3, w_e2):
    n, d = x.shape
    de = w_e1.shape[2]
    row = lambda v: v.reshape(1, -1)
    full = lambda a: pl.BlockSpec(a.shape, lambda e: (0,) * a.ndim)
    args = [att, glu, state_conv, x, gt1, sc2, sh2, gt2, conv_w, row(conv_b), row(ln_g), row(ln_b), w_out_bf16,
            row(g2), wr_hi, wr_lo, br, row(g_final)]
    return pl.pallas_call(
        _sample_tail_body,
        grid=(N_EXPERTS,),
        in_specs=[full(a) for a in args] + [pl.BlockSpec((1, d, de), lambda e: (e, 0, 0)),
                                            pl.BlockSpec((1, d, de), lambda e: (e, 0, 0)),
                                            pl.BlockSpec((1, de, d), lambda e: (e, 0, 0))],
        out_specs=pl.BlockSpec((n, d), lambda e: (0, 0)),
        out_shape=jax.ShapeDtypeStruct((n, d), F32),
        scratch_shapes=[pltpu.VMEM((n, d), F32), pltpu.VMEM((n, d), BF16),
                        pltpu.VMEM((n, ROUTE_LANES), F32), pltpu.VMEM((n, d), F32)],
        compiler_params=_cparams(("arbitrary",)),
        name="sample_tail_moe",
    )(*args, w_e1, w_e3, w_e2)


def _t5_bucket(dist):
    n = jnp.maximum(dist, 0)
    max_exact = NUM_BUCKETS // 2
    nf = jnp.maximum(n, 1).astype(F32)
    large = max_exact + (jnp.log(nf / max_exact) / math.log(MAX_DISTANCE / max_exact)
                         * (NUM_BUCKETS - max_exact)).astype(jnp.int32)
    large = jnp.minimum(large, NUM_BUCKETS - 1)
    return jnp.where(n < max_exact, n, large)


def prompt_bias_table(rel_bias, dilation):
    i = jnp.arange(ATT_BLK)[:, None]
    j = jnp.arange(2 * ATT_BLK)[None, :]
    rel = i + ATT_BLK - j
    return rel_bias[_t5_bucket(rel * dilation)].astype(F32).transpose(2, 0, 1)


def sample_bias_tables(rel_bias):
    dist = WIN_MAX - jnp.arange(WIN_MAX)
    slot_bias = rel_bias[_t5_bucket(dist)].astype(F32).T
    bias, valid, valid0 = [], [], []
    for window, d in DILATED_CFGS:
        member = (dist % d == 0) & (dist <= window) & ((PAST_LEN - dist) >= 0)
        bias.append(slot_bias)
        valid.append(jnp.broadcast_to(member.astype(F32)[None, :], (N_HEADS, WIN_MAX)))
        valid0.append(jnp.full((N_HEADS, LANES), float(PAST_LEN >= 0), F32))
    bias0 = jnp.broadcast_to(rel_bias[_t5_bucket(jnp.zeros((), jnp.int32))].astype(F32)[:, None], (N_HEADS, LANES))
    return jnp.stack(bias), jnp.stack(valid), bias0, jnp.stack(valid0)


def kernel(x_prompt, x_sample, cache_k, cache_v, state_conv, c_prompt, c_sample, rel_bias, w_ada, b_ada, g_norm1,
           w_in, conv_w, conv_b, ln_g, ln_b, w_out, g_norm2, w_router_group, b_router_group, w_router_expert,
           b_router_expert, w_expert_gate, w_expert_up, w_expert_down, g_final):
    depth = w_ada.shape[0]
    assert depth == 1, "single-layer step"
    bp, s, d = x_prompt.shape
    ns = x_sample.shape[0]
    w_in_b = w_in[0].astype(BF16)
    w_out_b = w_out[0].astype(BF16)
    wr_hi, wr_lo, br = router_weights(w_router_group[0], b_router_group[0], w_router_expert[0], b_router_expert[0])

    pad = (-(bp + ns)) % 8
    c_all = jnp.concatenate([c_prompt, c_sample, jnp.zeros((pad, d), c_prompt.dtype)], axis=0)
    mod = ada_modulation(c_all, w_ada[0], b_ada[0])
    mod_p = mod[:bp].reshape(bp, 1, 6 * d)
    sh1, sc1, gt1, sh2, sc2, gt2 = [mod_p[..., j * d:(j + 1) * d] for j in range(6)]
    mod_s = mod[bp:bp + ns]
    ssh1, ssc1, sgt1, ssh2, ssc2, sgt2 = [mod_s[:, j * d:(j + 1) * d] for j in range(6)]

    q, k, v, kc, vc, conv_act, tail = prompt_inproj(x_prompt, sc1, sh1, g_norm1[0], w_in_b, conv_w[0], conv_b[0],
                                                    ln_g[0], ln_b[0])
    o_list, lse_list = [], []
    for window, dil in DILATED_CFGS:
        o, lse = attention_branch(q, k, v, prompt_bias_table(rel_bias, dil), window, dil)
        o_list.append(o)
        lse_list.append(lse)
    x1, h2, route, counts = prompt_outproj(o_list, lse_list, conv_act, x_prompt, gt1, sc2, sh2, g_norm2[0], w_out_b,
                                           wr_hi, wr_lo, br)
    t = bp * s
    nblocks = (2 * t) // MOE_BLK + N_EXPERTS
    assert d == ROW_TILE * LANES
    dest, blk_e, nused, pend_ext = slot_plan(route, counts, nblocks)
    xs = moe_dispatch(h2.reshape(t, d), dest, pend_ext, nblocks * MOE_BLK)
    yb = moe_experts(xs, blk_e, nused, w_expert_gate[0], w_expert_up[0], w_expert_down[0])
    y_prompt = moe_combine(yb, dest, x1, route, gt2, g_final)

    xs2 = x_sample.reshape(ns, d)
    to_feature_major = lambda c: jnp.transpose(c, (0, 2, 3, 1)).reshape(c.shape[0], D_ATT, WIN_MAX)
    from_feature_major = lambda c: jnp.transpose(c.reshape(-1, N_HEADS, HEAD_DIM, WIN_MAX), (0, 3, 1, 2))[None]
    sq, sk, sv, sglu = sample_inproj(xs2, ssc1, ssh1, g_norm1[0], w_in_b)
    ko, vo, att_s = sample_cache_attention(sq, sk, sv, to_feature_major(cache_k[0]), to_feature_major(cache_v[0]),
                                           *sample_bias_tables(rel_bias))
    st_taps = jnp.transpose(state_conv[0], (1, 0, 2))
    y_s = sample_tail(att_s, sglu, st_taps, xs2, sgt1, ssc2, ssh2, sgt2, conv_w[0], conv_b[0], ln_g[0], ln_b[0],
                      w_out_b, g_norm2[0], wr_hi, wr_lo, br, g_final, w_expert_gate[0], w_expert_up[0],
                      w_expert_down[0])
    new_conv_s = jnp.transpose(jnp.concatenate([st_taps[1:], sglu[None]], axis=0), (1, 0, 2))

    return (y_prompt, y_s.reshape(ns, 1, d),
            from_feature_major(kc), from_feature_major(vc), tail[:, CONV_PAD - (CONV_W - 1):][None],
            from_feature_major(ko), from_feature_major(vo), new_conv_s[None])
```

```python
import functools
import math

import numpy as np
import jax
import jax.numpy as jnp
from jax import lax
from jax.experimental import pallas as pl
from jax.experimental.pallas import tpu as pltpu

F32 = jnp.float32
BF16 = jnp.bfloat16

D_MODEL = 1024
HEAD_DIM = 64
N_HEADS = 8
D_ATT = N_HEADS * HEAD_DIM
D_CONV = D_MODEL - D_ATT
DILATED_CFGS = ((128, 1), (512, 4), (2048, 16))
WIN_MAX = max(w for w, _ in DILATED_CFGS)
ATT_BLK = 128
CONV_W = 31
NUM_BUCKETS = 32
MAX_DISTANCE = WIN_MAX
N_GROUPS = 4
EXPERTS_PER_GROUP = 8
N_EXPERTS = N_GROUPS * EXPERTS_PER_GROUP
D_EXPERT = 512
EPS = 1e-6
NEG_INF = -1e30
PAST_LEN = 8192

LANES = 128
ROW_TILE = 8
CONV_PAD = 32
CONV_TAIL = 16
ROUTE_LANES = 128
GROUP_LANE0 = N_EXPERTS
MOE_BLK = 512
VMEM_LIMIT = 56 * 1024 * 1024


def _cparams(sem):
    return pltpu.CompilerParams(dimension_semantics=sem, vmem_limit_bytes=VMEM_LIMIT)


def _sigmoid(x):
    return 1.0 / (1.0 + jnp.exp(-x))


def _silu(x):
    return x * _sigmoid(x)


def _rms_scale(x):
    return x * lax.rsqrt(jnp.mean(x * x, axis=-1, keepdims=True) + EPS)


def _dot(a, b):
    return jnp.dot(a, b, preferred_element_type=F32)


def _ada_body(c_ref, w_ref, b_ref, o_ref):
    s = _silu(c_ref[...]).astype(BF16)
    o_ref[...] = _dot(s, w_ref[...].astype(BF16)) + b_ref[...]


def ada_modulation(c, w_ada, b_ada):
    rows, d = c.shape
    n = w_ada.shape[1]
    tn = 1024
    return pl.pallas_call(
        _ada_body,
        grid=(n // tn,),
        in_specs=[pl.BlockSpec((rows, d), lambda j: (0, 0)),
                  pl.BlockSpec((d, tn), lambda j: (0, j)),
                  pl.BlockSpec((1, tn), lambda j: (0, j))],
        out_specs=pl.BlockSpec((rows, tn), lambda j: (0, j)),
        out_shape=jax.ShapeDtypeStruct((rows, n), F32),
        compiler_params=_cparams(("arbitrary",)),
        name="ada_modulation",
    )(c, w_ada, b_ada.reshape(1, n))


def _conv_ln_swish(cbuf, tm, cw_ref, cb_ref, lg_ref, lb_ref, out_ref):
    chunk = 64
    base0 = CONV_PAD - (CONV_W - 1)
    for c in range(tm // chunk):
        pieces = []
        for lt in range(D_CONV // LANES):
            ls = slice(lt * LANES, (lt + 1) * LANES)
            acc = jnp.broadcast_to(cb_ref[:, ls], (chunk, LANES))
            for phase in range(ROW_TILE):
                part = None
                for j in range(CONV_W):
                    if (base0 + j) % ROW_TILE != phase:
                        continue
                    r0 = c * chunk + base0 + j - phase
                    term = cw_ref[j:j + 1, ls] * cbuf[r0:r0 + chunk + ROW_TILE, ls]
                    part = term if part is None else part + term
                if part is not None:
                    acc = acc + part[phase:phase + chunk, :]
            pieces.append(acc)
        acc = jnp.concatenate(pieces, axis=-1)
        mu = jnp.mean(acc, axis=-1, keepdims=True)
        xc = acc - mu
        var = jnp.mean(xc * xc, axis=-1, keepdims=True)
        y = xc * lax.rsqrt(var + EPS) * lg_ref[...] + lb_ref[...]
        out_ref[0, c * chunk:(c + 1) * chunk, :] = _silu(y).astype(out_ref.dtype)


def _inproj_body(x_ref, sc_ref, sh_ref, g_ref, w_ref, cw_ref, cb_ref, lg_ref, lb_ref,
                 q_ref, k_ref, v_ref, kc_ref, vc_ref, conv_ref, tail_ref, cbuf, *, first_cache_tile):
    i = pl.program_id(1)
    tm = x_ref.shape[1]

    @pl.when(i == 0)
    def _():
        cbuf[0:CONV_PAD, :] = jnp.zeros((CONV_PAD, D_CONV), F32)
        cbuf[CONV_PAD + tm:, :] = jnp.zeros((CONV_TAIL, D_CONV), F32)

    x = x_ref[0]
    h = (_rms_scale(x) * g_ref[...] * (1.0 + sc_ref[0]) + sh_ref[0]).astype(BF16)

    def col(j):
        return _dot(h, w_ref[:, j * D_ATT:(j + 1) * D_ATT])

    q_ref[0] = (col(0) * (HEAD_DIM ** -0.5)).astype(BF16)
    k = col(1)
    k_ref[0] = k.astype(BF16)
    v = col(2)
    v_ref[0] = v.astype(BF16)

    @pl.when(i >= first_cache_tile)
    def _():
        kc_ref[0] = k.T
        vc_ref[0] = v.T

    glu = col(3) * _sigmoid(col(4))
    cbuf[CONV_PAD:CONV_PAD + tm, :] = glu
    tail_ref[0] = glu[tm - CONV_PAD:, :]
    _conv_ln_swish(cbuf, tm, cw_ref, cb_ref, lg_ref, lb_ref, conv_ref)
    cbuf[0:CONV_PAD, :] = cbuf[tm:tm + CONV_PAD, :]


def prompt_inproj(x, sc1, sh1, g1, w_in_bf16, conv_w, conv_b, ln_g, ln_b, tm=512):
    b, s, d = x.shape
    nt = s // tm
    first_cache_tile = (s - WIN_MAX) // tm
    row = lambda v: v.reshape(1, -1)
    tok = lambda bi, i: (bi, i, 0)
    per_b = lambda bi, i: (bi, 0, 0)
    cst = lambda bi, i: (0, 0)
    cache = lambda bi, i: (bi, 0, jnp.maximum(i - first_cache_tile, 0))
    return pl.pallas_call(
        functools.partial(_inproj_body, first_cache_tile=first_cache_tile),
        grid=(b, nt),
        in_specs=[pl.BlockSpec((1, tm, d), tok),
                  pl.BlockSpec((1, 1, d), per_b),
                  pl.BlockSpec((1, 1, d), per_b),
                  pl.BlockSpec((1, d), cst),
                  pl.BlockSpec(w_in_bf16.shape, cst),
                  pl.BlockSpec(conv_w.shape, cst),
                  pl.BlockSpec((1, D_CONV), cst),
                  pl.BlockSpec((1, D_CONV), cst),
                  pl.BlockSpec((1, D_CONV), cst)],
        out_specs=[pl.BlockSpec((1, tm, D_ATT), tok),
                   pl.BlockSpec((1, tm, D_ATT), tok),
                   pl.BlockSpec((1, tm, D_ATT), tok),
                   pl.BlockSpec((1, D_ATT, tm), cache),
                   pl.BlockSpec((1, D_ATT, tm), cache),
                   pl.BlockSpec((1, tm, D_CONV), tok),
                   pl.BlockSpec((1, CONV_PAD, D_CONV), per_b)],
        out_shape=[jax.ShapeDtypeStruct((b, s, D_ATT), BF16),
                   jax.ShapeDtypeStruct((b, s, D_ATT), BF16),
                   jax.ShapeDtypeStruct((b, s, D_ATT), BF16),
                   jax.ShapeDtypeStruct((b, D_ATT, WIN_MAX), F32),
                   jax.ShapeDtypeStruct((b, D_ATT, WIN_MAX), F32),
                   jax.ShapeDtypeStruct((b, s, D_CONV), BF16),
                   jax.ShapeDtypeStruct((b, CONV_PAD, D_CONV), F32)],
        scratch_shapes=[pltpu.VMEM((tm + CONV_PAD + CONV_TAIL, D_CONV), F32)],
        compiler_params=_cparams(("arbitrary", "arbitrary")),
        name="prompt_inproj_conv",
    )(x, sc1, sh1, row(g1), w_in_bf16, conv_w, row(conv_b), row(ln_g), row(ln_b))


def _attn_body(q_ref, kc_ref, kp_ref, vc_ref, vp_ref, bias_ref, hmask_ref, o_ref, lse_ref, *, wsub):
    i = pl.program_id(2)
    nq = q_ref.shape[1] // ATT_BLK
    npair = N_HEADS // 2
    qi = lax.broadcasted_iota(jnp.int32, (ATT_BLK, 2 * ATT_BLK), 0)
    kj = lax.broadcasted_iota(jnp.int32, (ATT_BLK, 2 * ATT_BLK), 1)
    rel = qi + ATT_BLK - kj
    band = (rel >= 0) & (rel <= wsub)
    band_first = band & ((kj >= ATT_BLK) | (i > 0))
    low = lax.broadcasted_iota(jnp.int32, (ATT_BLK, LANES), 1) < HEAD_DIM
    for sub in range(nq):
        r0 = sub * ATT_BLK
        valid = band_first if sub == 0 else band
        scores, values = [], []
        for hp in range(npair):
            cs = slice(hp * LANES, (hp + 1) * LANES)
            qp = q_ref[0, r0:r0 + ATT_BLK, cs]
            if sub == 0:
                kk = jnp.concatenate([kp_ref[0, :, cs], kc_ref[0, 0:ATT_BLK, cs]], axis=0)
                vv = jnp.concatenate([vp_ref[0, :, cs], vc_ref[0, 0:ATT_BLK, cs]], axis=0)
            else:
                kk = kc_ref[0, r0 - ATT_BLK:r0 + ATT_BLK, cs]
                vv = vc_ref[0, r0 - ATT_BLK:r0 + ATT_BLK, cs]
            values.append(vv)
            for half in range(2):
                s = lax.dot_general(qp * hmask_ref[half], kk, (((1,), (1,)), ((), ())), preferred_element_type=F32)
                scores.append(jnp.where(valid, s + bias_ref[2 * hp + half], NEG_INF))
        s = jnp.concatenate(scores, axis=0)
        m = jnp.max(s, axis=-1, keepdims=True)
        p = jnp.exp(s - m)
        l = jnp.sum(p, axis=-1, keepdims=True)
        p16 = p.astype(BF16)
        inv = 1.0 / l
        lse = m + jnp.log(l)
        for hp in range(npair):
            cs = slice(hp * LANES, (hp + 1) * LANES)
            ra = slice(2 * hp * ATT_BLK, (2 * hp + 1) * ATT_BLK)
            rb = slice((2 * hp + 1) * ATT_BLK, (2 * hp + 2) * ATT_BLK)
            oa = _dot(p16[ra], values[hp]) * inv[ra]
            ob = _dot(p16[rb], values[hp]) * inv[rb]
            o_ref[0, r0:r0 + ATT_BLK, cs] = jnp.where(low, oa, ob).astype(o_ref.dtype)
            lse_ref[0, r0:r0 + ATT_BLK, cs] = jnp.where(low, lse[ra], lse[rb])


def attention_branch(q, k, v, bias, window, dilation, qblocks=4):
    b, s, c = q.shape
    n = s // dilation
    wsub = window // dilation
    tq = ATT_BLK * qblocks
    view = lambda a: a.reshape(b, n, dilation * c)
    cur = lambda bi, r, i: (bi, i, r)
    prev = lambda bi, r, i: (bi, jnp.maximum(i * qblocks - 1, 0), r)
    lane = jnp.arange(LANES)[None, :]
    hmask = jnp.broadcast_to(jnp.stack([lane < HEAD_DIM, lane >= HEAD_DIM]).astype(BF16), (2, ATT_BLK, LANES))
    o, lse = pl.pallas_call(
        functools.partial(_attn_body, wsub=wsub),
        grid=(b, dilation, n // tq),
        in_specs=[pl.BlockSpec((1, tq, c), cur),
                  pl.BlockSpec((1, tq, c), cur),
                  pl.BlockSpec((1, ATT_BLK, c), prev),
                  pl.BlockSpec((1, tq, c), cur),
                  pl.BlockSpec((1, ATT_BLK, c), prev),
                  pl.BlockSpec(bias.shape, lambda bi, r, i: (0, 0, 0)),
                  pl.BlockSpec(hmask.shape, lambda bi, r, i: (0, 0, 0))],
        out_specs=[pl.BlockSpec((1, tq, c), cur),
                   pl.BlockSpec((1, tq, c), cur)],
        out_shape=[jax.ShapeDtypeStruct((b, n, dilation * c), BF16),
                   jax.ShapeDtypeStruct((b, n, dilation * c), F32)],
        compiler_params=_cparams(("arbitrary", "arbitrary", "arbitrary")),
        name=f"attn_branch_d{dilation}",
    )(view(q), view(k), view(k), view(v), view(v), bias, hmask)
    return o.reshape(b, s, c), lse.reshape(b, s, c)


def _split_bf16(a):
    hi = a.astype(BF16)
    lo = (a - hi.astype(F32)).astype(BF16)
    return hi, lo


def _router_logits(h2, wr_hi_ref, wr_lo_ref, br_ref):
    hi, lo = _split_bf16(h2)
    return _dot(hi, wr_hi_ref[...]) + (_dot(hi, wr_lo_ref[...]) + _dot(lo, wr_hi_ref[...])) + br_ref[...]


def _route(logits):
    rows = logits.shape[0]
    lane = lax.broadcasted_iota(jnp.int32, (rows, ROUTE_LANES), 1)
    lanef = lane.astype(F32)
    big = float(ROUTE_LANES)
    is_g = (lane >= GROUP_LANE0) & (lane < GROUP_LANE0 + N_GROUPS)
    glog = jnp.where(is_g, logits, -jnp.inf)
    gmax = jnp.max(glog, axis=-1, keepdims=True)
    gsum = jnp.sum(jnp.exp(glog - gmax), axis=-1, keepdims=True)
    p_grp = 1.0 / gsum
    glane = jnp.min(jnp.where(glog == gmax, lanef, big), axis=-1, keepdims=True)
    grp = glane - float(GROUP_LANE0)
    in_grp = jnp.floor(lanef * (1.0 / EXPERTS_PER_GROUP)) == grp
    elog = jnp.where(in_grp, logits, -jnp.inf)
    emax = jnp.max(elog, axis=-1, keepdims=True)
    eexp = jnp.exp(elog - emax)
    pe = eexp / jnp.sum(eexp, axis=-1, keepdims=True)
    pe = jnp.where(in_grp, pe, -1.0)
    p1 = jnp.max(pe, axis=-1, keepdims=True)
    e1 = jnp.min(jnp.where(pe == p1, lanef, big), axis=-1, keepdims=True)
    pe2 = jnp.where(lanef == e1, -1.0, pe)
    p2 = jnp.max(pe2, axis=-1, keepdims=True)
    e2 = jnp.min(jnp.where(pe2 == p2, lanef, big), axis=-1, keepdims=True)
    psum = p1 + p2
    return e1, e2, p_grp * p1 / psum, p_grp * p2 / psum


def _mix_branches(o_refs, lse_refs):
    lses = [r[0] for r in lse_refs]
    mx = jnp.maximum(jnp.maximum(lses[0], lses[1]), lses[2])
    ws = [jnp.exp(l - mx) for l in lses]
    numer = sum(w * o[0].astype(F32) for w, o in zip(ws, o_refs))
    return numer / (ws[0] + ws[1] + ws[2])


def _outproj_body(o1, o4, o16, l1, l4, l16, conv_ref, x_ref, gt_ref, sc_ref, sh_ref, g_ref,
                  wo_ref, wrh_ref, wrl_ref, br_ref,
                  x1_ref, h2_ref, route_ref, cnt_ref, carry):
    first = (pl.program_id(0) == 0) & (pl.program_id(1) == 0)

    @pl.when(first)
    def _():
        carry[...] = jnp.zeros_like(carry)

    tm = x_ref.shape[1]
    att = _mix_branches((o1, o4, o16), (l1, l4, l16)).astype(BF16)
    mix = _dot(att, wo_ref[0:D_ATT, :]) + _dot(conv_ref[0], wo_ref[D_ATT:, :])
    x1 = x_ref[0] + gt_ref[0] * mix
    x1_ref[0] = x1
    h2 = _rms_scale(x1) * g_ref[...] * (1.0 + sc_ref[0]) + sh_ref[0]
    h2_ref[0] = h2.astype(BF16)
    e1, e2, w1, w2 = _route(_router_logits(h2, wrh_ref, wrl_ref, br_ref))

    lane = lax.broadcasted_iota(jnp.int32, (tm, ROUTE_LANES), 1).astype(F32)
    oh1 = lane == e1
    oh2 = lane == e2
    onehot = jnp.where(oh1 | oh2, 1.0, 0.0)
    ri = lax.broadcasted_iota(jnp.int32, (tm, tm), 0)
    ci = lax.broadcasted_iota(jnp.int32, (tm, tm), 1)
    tri = jnp.where(ci < ri, 1.0, 0.0).astype(BF16)
    before = _dot(tri, onehot.astype(BF16)) + carry[...]
    r1 = jnp.sum(jnp.where(oh1, before, 0.0), axis=-1, keepdims=True)
    r2 = jnp.sum(jnp.where(oh2, before, 0.0), axis=-1, keepdims=True)
    carry[...] = carry[...] + jnp.sum(onehot, axis=0, keepdims=True)
    cnt_ref[...] = carry[...]
    rec = jnp.where(lane == 0.0, e1, 0.0)
    for idx, val in ((1, e2), (2, w1), (3, w2), (4, r1), (5, r2)):
        rec = jnp.where(lane == float(idx), val, rec)
    route_ref[0] = rec


def prompt_outproj(o_list, lse_list, conv_act, x, gt1, sc2, sh2, g2, w_out_bf16, wr_hi, wr_lo, br, tm=512):
    b, s, d = x.shape
    tok = lambda bi, i: (bi, i, 0)
    per_b = lambda bi, i: (bi, 0, 0)
    cst = lambda bi, i: (0, 0)
    half = pl.BlockSpec((1, tm, D_ATT), tok)
    return pl.pallas_call(
        _outproj_body,
        grid=(b, s // tm),
        in_specs=[half] * 7 + [pl.BlockSpec((1, tm, d), tok),
                               pl.BlockSpec((1, 1, d), per_b),
                               pl.BlockSpec((1, 1, d), per_b),
                               pl.BlockSpec((1, 1, d), per_b),
                               pl.BlockSpec((1, d), cst),
                               pl.BlockSpec(w_out_bf16.shape, cst),
                               pl.BlockSpec(wr_hi.shape, cst),
                               pl.BlockSpec(wr_lo.shape, cst),
                               pl.BlockSpec((1, ROUTE_LANES), cst)],
        out_specs=[pl.BlockSpec((1, tm, d), tok),
                   pl.BlockSpec((1, tm, d), tok),
                   pl.BlockSpec((1, tm, ROUTE_LANES), tok),
                   pl.BlockSpec((1, ROUTE_LANES), cst)],
        out_shape=[jax.ShapeDtypeStruct((b, s, d), F32),
                   jax.ShapeDtypeStruct((b, s, d), BF16),
                   jax.ShapeDtypeStruct((b, s, ROUTE_LANES), F32),
                   jax.ShapeDtypeStruct((1, ROUTE_LANES), F32)],
        scratch_shapes=[pltpu.VMEM((1, ROUTE_LANES), F32)],
        compiler_params=_cparams(("arbitrary", "arbitrary")),
        name="prompt_outproj_router",
    )(*o_list, *lse_list, conv_act, x, gt1, sc2, sh2, g2.reshape(1, d), w_out_bf16, wr_hi, wr_lo, br)


def router_weights(w_rg, b_rg, w_re, b_re):
    d = w_rg.shape[0]
    fill = ROUTE_LANES - N_EXPERTS - N_GROUPS
    w = jnp.concatenate([w_re.astype(F32), w_rg.astype(F32), jnp.zeros((d, fill), F32)], axis=1)
    bias = jnp.concatenate([b_re.astype(F32), b_rg.astype(F32), jnp.zeros((fill,), F32)])[None, :]
    hi = w.astype(BF16)
    lo = (w - hi.astype(F32)).astype(BF16)
    return hi, lo, bias


def _rows_from_tiles(tile_ref, n):
    return jnp.concatenate([tile_ref[pl.ds(c, n, stride=ROW_TILE), :] for c in range(ROW_TILE)], axis=-1)


def _rows_to_tiles(tile_ref, rows):
    n = rows.shape[0]
    for c in range(ROW_TILE):
        tile_ref[pl.ds(c, n, stride=ROW_TILE), :] = rows[:, c * LANES:(c + 1) * LANES]


def _dispatch_body(dest_ref, pend_ref, h2_ref, xs_hbm, tiles, zbuf, sem, zsem, *, chunk):
    step = pl.program_id(0)

    @pl.when(step == 0)
    def _():
        zbuf[...] = jnp.zeros_like(zbuf)

        def zcopy(e):
            start = pl.multiple_of(jnp.maximum(pend_ref[e + 1] - MOE_BLK * ROW_TILE, 0), ROW_TILE)
            return pltpu.make_async_copy(zbuf, xs_hbm.at[pl.ds(start, MOE_BLK * ROW_TILE)], zsem)

        def zstart(e, c):
            @pl.when(pend_ref[e + 1] > pend_ref[e])
            def _():
                zcopy(e).start()
            return c

        def zwait(e, c):
            @pl.when(pend_ref[e + 1] > pend_ref[e])
            def _():
                zcopy(e).wait()
            return c

        lax.fori_loop(0, N_EXPERTS, zstart, 0)
        lax.fori_loop(0, N_EXPERTS, zwait, 0)

        blk_rows = MOE_BLK * ROW_TILE
        nblk = xs_hbm.shape[0] // blk_rows

        def tcopy(i):
            return pltpu.make_async_copy(zbuf, xs_hbm.at[pl.ds(pl.multiple_of(i * blk_rows, blk_rows), blk_rows)],
                                         zsem)

        def tstart(i, c):
            @pl.when(i * blk_rows >= pend_ref[N_EXPERTS])
            def _():
                tcopy(i).start()
            return c

        def twait(i, c):
            @pl.when(i * blk_rows >= pend_ref[N_EXPERTS])
            def _():
                tcopy(i).wait()
            return c

        lax.fori_loop(0, nblk, tstart, 0)
        lax.fori_loop(0, nblk, twait, 0)

    _rows_to_tiles(tiles, h2_ref[...].astype(F32))

    def copy(t, k):
        src = pl.multiple_of(t * ROW_TILE, ROW_TILE)
        dst = pl.multiple_of(dest_ref[0, 0, 2 * t + k], ROW_TILE)
        return pltpu.make_async_copy(tiles.at[pl.ds(src, ROW_TILE)], xs_hbm.at[pl.ds(dst, ROW_TILE)], sem)

    def start(t, c):
        copy(t, 0).start()
        copy(t, 1).start()
        return c

    def wait(t, c):
        copy(t, 0).wait()
        copy(t, 1).wait()
        return c

    lax.fori_loop(0, chunk, start, 0, unroll=8)
    lax.fori_loop(0, chunk, wait, 0, unroll=8)


def moe_dispatch(h2_flat, dest_rt, pends_rt, nslots, chunk=512):
    t, d = h2_flat.shape
    assert d == ROW_TILE * LANES
    nsteps = t // chunk
    dest3 = dest_rt.reshape(nsteps, 1, 2 * chunk)
    return pl.pallas_call(
        functools.partial(_dispatch_body, chunk=chunk),
        grid=(nsteps,),
        in_specs=[pl.BlockSpec((1, 1, 2 * chunk), lambda i: (i, 0, 0), memory_space=pltpu.SMEM),
                  pl.BlockSpec(memory_space=pltpu.SMEM),
                  pl.BlockSpec((chunk, d), lambda i: (i, 0))],
        out_specs=pl.BlockSpec(memory_space=pl.ANY),
        out_shape=jax.ShapeDtypeStruct((nslots * ROW_TILE, LANES), F32),
        scratch_shapes=[pltpu.VMEM((chunk * ROW_TILE, LANES), F32),
                        pltpu.VMEM((MOE_BLK * ROW_TILE, LANES), F32),
                        pltpu.SemaphoreType.DMA(()),
                        pltpu.SemaphoreType.DMA(())],
        compiler_params=_cparams(("arbitrary",)),
        name="moe_dispatch",
    )(dest3, pends_rt, h2_flat)


def _experts_body(blk_e_ref, nused_ref, xs_ref, w1_ref, w3_ref, w2_ref, y_ref, w1b, w3b, w2b):
    i = pl.program_id(0)
    new_expert = (i == 0) | (blk_e_ref[i] != blk_e_ref[jnp.maximum(i - 1, 0)])

    @pl.when((i < nused_ref[0]) & new_expert)
    def _():
        w1b[...] = w1_ref[0].astype(BF16)
        w3b[...] = w3_ref[0].astype(BF16)
        w2b[...] = w2_ref[0].astype(BF16)

    @pl.when(i < nused_ref[0])
    def _():
        xb = _rows_from_tiles(xs_ref, MOE_BLK).astype(BF16)
        a = (_silu(_dot(xb, w1b[...])) * _dot(xb, w3b[...])).astype(BF16)
        _rows_to_tiles(y_ref, _dot(a, w2b[...]))

    @pl.when(i >= nused_ref[0])
    def _():
        y_ref[...] = jnp.zeros_like(y_ref)


def moe_experts(xs, blk_e, nused, w_e1, w_e3, w_e2):
    nb = xs.shape[0] // (MOE_BLK * ROW_TILE)
    _, d, de = w_e1.shape
    blk = lambda i, be, nu: (jnp.minimum(i, nu[0] - 1), 0)
    wsel = lambda i, be, nu: (be[i], 0, 0)
    grid_spec = pltpu.PrefetchScalarGridSpec(
        num_scalar_prefetch=2,
        grid=(nb,),
        in_specs=[pl.BlockSpec((MOE_BLK * ROW_TILE, LANES), blk),
                  pl.BlockSpec((1, d, de), wsel),
                  pl.BlockSpec((1, d, de), wsel),
                  pl.BlockSpec((1, de, d), wsel)],
        out_specs=pl.BlockSpec((MOE_BLK * ROW_TILE, LANES), lambda i, be, nu: (i, 0)),
        scratch_shapes=[pltpu.VMEM((d, de), BF16), pltpu.VMEM((d, de), BF16), pltpu.VMEM((de, d), BF16)],
    )
    return pl.pallas_call(
        _experts_body,
        grid_spec=grid_spec,
        out_shape=jax.ShapeDtypeStruct(xs.shape, F32),
        compiler_params=_cparams(("arbitrary",)),
        name="moe_experts",
    )(blk_e, nused, xs, w_e1, w_e3, w_e2)


def _combine_body(dest_ref, yb_hbm, x1_ref, route_ref, gt_ref, gf_ref, out_ref, ya0, ya1, sem, *, tm):
    def copy(t, k, buf):
        src = pl.multiple_of(dest_ref[0, 0, 2 * t + k], ROW_TILE)
        dst = pl.multiple_of(t * ROW_TILE, ROW_TILE)
        return pltpu.make_async_copy(yb_hbm.at[pl.ds(src, ROW_TILE)], buf.at[pl.ds(dst, ROW_TILE)], sem)

    def start(t, c):
        copy(t, 0, ya0).start()
        copy(t, 1, ya1).start()
        return c

    def wait(t, c):
        copy(t, 0, ya0).wait()
        copy(t, 1, ya1).wait()
        return c

    lax.fori_loop(0, tm, start, 0, unroll=8)
    lax.fori_loop(0, tm, wait, 0, unroll=8)
    rec = route_ref[0]
    moe = rec[:, 2:3] * _rows_from_tiles(ya0, tm) + rec[:, 3:4] * _rows_from_tiles(ya1, tm)
    y = x1_ref[0] + gt_ref[0] * moe
    out_ref[0] = _rms_scale(y) * gf_ref[...]


def moe_combine(yb, dest_rt, x1, route, gt2, g_final, tm=256):
    b, s, d = x1.shape
    nt = s // tm
    dest3 = dest_rt.reshape(b * nt, 1, 2 * tm)
    tok = lambda bi, i: (bi, i, 0)
    return pl.pallas_call(
        functools.partial(_combine_body, tm=tm),
        grid=(b, nt),
        in_specs=[pl.BlockSpec((1, 1, 2 * tm), lambda bi, i: (bi * nt + i, 0, 0), memory_space=pltpu.SMEM),
                  pl.BlockSpec(memory_space=pl.ANY),
                  pl.BlockSpec((1, tm, d), tok),
                  pl.BlockSpec((1, tm, ROUTE_LANES), tok),
                  pl.BlockSpec((1, 1, d), lambda bi, i: (bi, 0, 0)),
                  pl.BlockSpec((1, d), lambda bi, i: (0, 0))],
        out_specs=pl.BlockSpec((1, tm, d), tok),
        out_shape=jax.ShapeDtypeStruct((b, s, d), F32),
        scratch_shapes=[pltpu.VMEM((tm * ROW_TILE, LANES), F32), pltpu.VMEM((tm * ROW_TILE, LANES), F32),
                        pltpu.SemaphoreType.DMA(())],
        compiler_params=_cparams(("arbitrary", "arbitrary")),
        name="moe_combine_final",
    )(dest3, yb, x1, route, gt2, g_final.reshape(1, d))


def slot_plan(route, counts, nblocks):
    cnt = counts[0, :N_EXPERTS].astype(jnp.int32)
    padded = (cnt + MOE_BLK - 1) // MOE_BLK * MOE_BLK
    pends = jnp.cumsum(padded)
    pstarts = pends - padded
    eid = route[..., 0:2].astype(jnp.int32)
    rank = route[..., 4:6].astype(jnp.int32)
    experts = jnp.arange(N_EXPERTS, dtype=jnp.int32)
    start_of = jnp.sum(jnp.where(eid[..., None] == experts, pstarts, 0), axis=-1)
    dest = (start_of + rank).reshape(-1)
    blk_start = jnp.arange(nblocks, dtype=jnp.int32) * MOE_BLK
    blk_e = jnp.minimum(jnp.sum((pends[None, :] <= blk_start[:, None]).astype(jnp.int32), axis=1), N_EXPERTS - 1)
    nused = (pends[-1:] // MOE_BLK).astype(jnp.int32)
    pend_ext = jnp.concatenate([jnp.zeros((1,), jnp.int32), pends.astype(jnp.int32)])
    return (dest * ROW_TILE).astype(jnp.int32), blk_e, nused, (pend_ext * ROW_TILE).astype(jnp.int32)


def _sample_inproj_body(x_ref, sc_ref, sh_ref, g_ref, w_ref, q_ref, k_ref, v_ref, glu_ref):
    h = (_rms_scale(x_ref[...]) * g_ref[...] * (1.0 + sc_ref[...]) + sh_ref[...]).astype(BF16)

    def col(j):
        return _dot(h, w_ref[:, j * D_ATT:(j + 1) * D_ATT])

    q_ref[...] = col(0) * (HEAD_DIM ** -0.5)
    k_ref[...] = col(1)
    v_ref[...] = col(2)
    glu_ref[...] = col(3) * _sigmoid(col(4))


def sample_inproj(x, sc1, sh1, g1, w_in_bf16):
    n, d = x.shape
    out = jax.ShapeDtypeStruct((n, D_ATT), F32)
    return pl.pallas_call(
        _sample_inproj_body,
        out_shape=[out, out, out, out],
        compiler_params=pltpu.CompilerParams(vmem_limit_bytes=VMEM_LIMIT),
        name="sample_inproj",
    )(x, sc1, sh1, g1.reshape(1, d), w_in_bf16)


def _sample_cache_body(qt_ref, knt_ref, vnt_ref, k_ref, v_ref, bias_ref, valid_ref, bias0_ref, valid0_ref,
                       ko_ref, vo_ref, att_ref, s_s, p_s, n_s):
    b = pl.program_id(0)
    win = k_ref.shape[2]
    sel = lax.broadcasted_iota(jnp.int32, qt_ref.shape, 1) == b

    def column(ref):
        return jnp.sum(jnp.where(sel, ref[...], 0.0), axis=1, keepdims=True)

    q, kn, vn = column(qt_ref), column(knt_ref), column(vnt_ref)
    kt = k_ref[0]
    vt = v_ref[0]
    for h in range(N_HEADS):
        rows = slice(h * HEAD_DIM, (h + 1) * HEAD_DIM)
        s_s[h:h + 1, :] = jnp.sum(kt[rows, :] * q[rows, :], axis=0, keepdims=True)
        n_s[h:h + 1, :] = jnp.broadcast_to(jnp.sum(kn[rows, :] * q[rows, :], axis=0, keepdims=True), (1, LANES))
    s_all = s_s[...]
    s_new = n_s[...][:, 0:1] + bias0_ref[:, 0:1]
    parts = []
    for br in range(len(DILATED_CFGS)):
        s = jnp.where(valid_ref[br] > 0.0, s_all + bias_ref[br], NEG_INF)
        s0 = jnp.where(valid0_ref[br][:, 0:1] > 0.0, s_new, NEG_INF)
        m = jnp.maximum(jnp.max(s, axis=-1, keepdims=True), s0)
        p = jnp.exp(s - m)
        p0 = jnp.exp(s0 - m)
        parts.append((p, p0, m, jnp.sum(p, axis=-1, keepdims=True) + p0))
    mx = jnp.maximum(jnp.maximum(parts[0][2], parts[1][2]), parts[2][2])
    ws = [jnp.exp(m - mx) for _, _, m, _ in parts]
    inv = 1.0 / sum(w * l for w, (_, _, _, l) in zip(ws, parts))
    p_s[...] = sum(w * p for w, (p, _, _, _) in zip(ws, parts)) * inv
    p_new = sum(w * p0 for w, (_, p0, _, _) in zip(ws, parts)) * inv
    cols = []
    for h in range(N_HEADS):
        rows = slice(h * HEAD_DIM, (h + 1) * HEAD_DIM)
        cols.append(jnp.sum(vt[rows, :] * p_s[h:h + 1, :], axis=1, keepdims=True) + p_new[h:h + 1, :] * vn[rows, :])
    att = jnp.concatenate(cols, axis=0)

    @pl.when(b == 0)
    def _():
        att_ref[...] = jnp.zeros_like(att_ref)

    att_ref[...] = jnp.where(sel, att, att_ref[...])
    last = lax.broadcasted_iota(jnp.int32, kt.shape, 1) == win - 1
    ko_ref[0] = jnp.where(last, kn, pltpu.roll(kt, win - 1, 1))
    vo_ref[0] = jnp.where(last, vn, pltpu.roll(vt, win - 1, 1))


def sample_cache_attention(q, k_new, v_new, cache_kt, cache_vt, bias, valid, bias0, valid0):
    n, c, win = cache_kt.shape
    full = lambda a: pl.BlockSpec(a.shape, lambda i: (0,) * a.ndim)
    seq = pl.BlockSpec((1, c, win), lambda i: (i, 0, 0))
    qt, knt, vnt = q.T, k_new.T, v_new.T
    ko, vo, att_t = pl.pallas_call(
        _sample_cache_body,
        grid=(n,),
        in_specs=[full(qt), full(knt), full(vnt), seq, seq, full(bias), full(valid), full(bias0), full(valid0)],
        out_specs=[seq, seq, pl.BlockSpec((c, n), lambda i: (0, 0))],
        out_shape=[jax.ShapeDtypeStruct(cache_kt.shape, F32), jax.ShapeDtypeStruct(cache_vt.shape, F32),
                   jax.ShapeDtypeStruct((c, n), F32)],
        scratch_shapes=[pltpu.VMEM((N_HEADS, win), F32), pltpu.VMEM((N_HEADS, win), F32),
                        pltpu.VMEM((N_HEADS, LANES), F32)],
        compiler_params=_cparams(("arbitrary",)),
        name="sample_cache_attention",
    )(qt, knt, vnt, cache_kt, cache_vt, bias, valid, bias0, valid0)
    return ko, vo, att_t.T


def _sample_tail_body(att_ref, glu_ref, st_ref, x_ref, gt1_ref, sc2_ref, sh2_ref, gt2_ref,
                      cw_ref, cb_ref, lg_ref, lb_ref, wo_ref, g2_ref, wrh_ref, wrl_ref, br_ref, gf_ref,
                      w1_ref, w3_ref, w2_ref, y_ref, x1_s, h2_s, gate_s, acc_s):
    e = pl.program_id(0)
    n = x_ref.shape[0]

    @pl.when(e == 0)
    def _():
        nhist = CONV_W - 1
        conv = cw_ref[nhist:CONV_W, :] * glu_ref[...] + cb_ref[...]
        for j in range(nhist):
            conv = conv + cw_ref[j:j + 1, :] * st_ref[j]
        mu = jnp.mean(conv, axis=-1, keepdims=True)
        xc = conv - mu
        var = jnp.mean(xc * xc, axis=-1, keepdims=True)
        act = _silu(xc * lax.rsqrt(var + EPS) * lg_ref[...] + lb_ref[...]).astype(BF16)
        mix = _dot(att_ref[...].astype(BF16), wo_ref[0:D_ATT, :]) + _dot(act, wo_ref[D_ATT:, :])
        x1 = x_ref[...] + gt1_ref[...] * mix
        x1_s[...] = x1
        h2 = _rms_scale(x1) * g2_ref[...] * (1.0 + sc2_ref[...]) + sh2_ref[...]
        h2_s[...] = h2.astype(BF16)
        e1, e2, w1, w2 = _route(_router_logits(h2, wrh_ref, wrl_ref, br_ref))
        lane = lax.broadcasted_iota(jnp.int32, (n, ROUTE_LANES), 1).astype(F32)
        gate_s[...] = jnp.where(lane == e1, w1, 0.0) + jnp.where(lane == e2, w2, 0.0)
        acc_s[...] = jnp.zeros_like(acc_s)

    hb = h2_s[...]
    a = (_silu(_dot(hb, w1_ref[0].astype(BF16))) * _dot(hb, w3_ref[0].astype(BF16))).astype(BF16)
    ye = _dot(a, w2_ref[0].astype(BF16))
    lane = lax.broadcasted_iota(jnp.int32, (n, ROUTE_LANES), 1)
    g = jnp.sum(jnp.where(lane == e, gate_s[...], 0.0), axis=-1, keepdims=True)
    acc_s[...] = acc_s[...] + g * ye

    @pl.when(e == pl.num_programs(0) - 1)
    def _():
        y = x1_s[...] + gt2_ref[...] * acc_s[...]
        y_ref[...] = _rms_scale(y) * gf_ref[...]


def sample_tail(att, glu, state_conv, x, gt1, sc2, sh2, gt2, conv_w, conv_b, ln_g, ln_b, w_out_bf16, g2,
                wr_hi, wr_lo, br, g_final, w_e1, w_e3, w_e2):
    n, d = x.shape
    de = w_e1.shape[2]
    row = lambda v: v.reshape(1, -1)
    full = lambda a: pl.BlockSpec(a.shape, lambda e: (0,) * a.ndim)
    args = [att, glu, state_conv, x, gt1, sc2, sh2, gt2, conv_w, row(conv_b), row(ln_g), row(ln_b), w_out_bf16,
            row(g2), wr_hi, wr_lo, br, row(g_final)]
    return pl.pallas_call(
        _sample_tail_body,
        grid=(N_EXPERTS,),
        in_specs=[full(a) for a in args] + [pl.BlockSpec((1, d, de), lambda e: (e, 0, 0)),
                                            pl.BlockSpec((1, d, de), lambda e: (e, 0, 0)),
                                            pl.BlockSpec((1, de, d), lambda e: (e, 0, 0))],
        out_specs=pl.BlockSpec((n, d), lambda e: (0, 0)),
        out_shape=jax.ShapeDtypeStruct((n, d), F32),
        scratch_shapes=[pltpu.VMEM((n, d), F32), pltpu.VMEM((n, d), BF16),
                        pltpu.VMEM((n, ROUTE_LANES), F32), pltpu.VMEM((n, d), F32)],
        compiler_params=_cparams(("arbitrary",)),
        name="sample_tail_moe",
    )(*args, w_e1, w_e3, w_e2)


def _t5_bucket_static(dist):
    n = np.maximum(np.asarray(dist, np.int64), 0)
    max_exact = NUM_BUCKETS // 2
    x = np.log(np.maximum(n, 1) / max_exact) / math.log(MAX_DISTANCE / max_exact) * (NUM_BUCKETS - max_exact)
    on_edge = (np.abs(x - np.rint(x)) < 2e-5) & (x > 0.5) & (np.rint(x) < NUM_BUCKETS - max_exact)
    assert not on_edge.any()
    large = np.minimum(max_exact + np.floor(x + 1e-9).astype(np.int64), NUM_BUCKETS - 1)
    return np.where(n < max_exact, n, large)


def _bias_lookup(rel_bias, dist):
    bucket = _t5_bucket_static(dist)
    onehot = (bucket.reshape(-1, 1) == np.arange(NUM_BUCKETS)[None, :]).astype(np.float32)
    table = jnp.einsum('nb,bh->hn', onehot, rel_bias.astype(F32), precision=lax.Precision.HIGHEST)
    return table.reshape((rel_bias.shape[1],) + bucket.shape)


def prompt_bias_table(rel_bias, dilation):
    i = np.arange(ATT_BLK)[:, None]
    j = np.arange(2 * ATT_BLK)[None, :]
    return _bias_lookup(rel_bias, (i + ATT_BLK - j) * dilation)


def sample_bias_tables(rel_bias):
    dist = WIN_MAX - np.arange(WIN_MAX)
    slot_bias = _bias_lookup(rel_bias, dist)
    bias, valid, valid0 = [], [], []
    for window, d in DILATED_CFGS:
        member = (dist % d == 0) & (dist <= window) & ((PAST_LEN - dist) >= 0)
        bias.append(slot_bias)
        valid.append(jnp.asarray(np.broadcast_to(member.astype(np.float32)[None, :], (N_HEADS, WIN_MAX))))
        valid0.append(jnp.full((N_HEADS, LANES), float(PAST_LEN >= 0), F32))
    bias0 = jnp.broadcast_to(_bias_lookup(rel_bias, np.zeros((1,), np.int64)), (N_HEADS, LANES))
    return jnp.stack(bias), jnp.stack(valid), bias0, jnp.stack(valid0)


def kernel(x_prompt, x_sample, cache_k, cache_v, state_conv, c_prompt, c_sample, rel_bias, w_ada, b_ada, g_norm1,
           w_in, conv_w, conv_b, ln_g, ln_b, w_out, g_norm2, w_router_group, b_router_group, w_router_expert,
           b_router_expert, w_expert_gate, w_expert_up, w_expert_down, g_final):
    depth = w_ada.shape[0]
    assert depth == 1, "single-layer step"
    bp, s, d = x_prompt.shape
    ns = x_sample.shape[0]
    w_in_b = w_in[0].astype(BF16)
    w_out_b = w_out[0].astype(BF16)
    wr_hi, wr_lo, br = router_weights(w_router_group[0], b_router_group[0], w_router_expert[0], b_router_expert[0])

    pad = (-(bp + ns)) % 8
    c_all = jnp.concatenate([c_prompt, c_sample, jnp.zeros((pad, d), c_prompt.dtype)], axis=0)
    mod = ada_modulation(c_all, w_ada[0], b_ada[0])
    mod_p = mod[:bp].reshape(bp, 1, 6 * d)
    sh1, sc1, gt1, sh2, sc2, gt2 = [mod_p[..., j * d:(j + 1) * d] for j in range(6)]
    mod_s = mod[bp:bp + ns]
    ssh1, ssc1, sgt1, ssh2, ssc2, sgt2 = [mod_s[:, j * d:(j + 1) * d] for j in range(6)]

    q, k, v, kc, vc, conv_act, tail = prompt_inproj(x_prompt, sc1, sh1, g_norm1[0], w_in_b, conv_w[0], conv_b[0],
                                                    ln_g[0], ln_b[0])
    o_list, lse_list = [], []
    for window, dil in DILATED_CFGS:
        o, lse = attention_branch(q, k, v, prompt_bias_table(rel_bias, dil), window, dil)
        o_list.append(o)
        lse_list.append(lse)
    x1, h2, route, counts = prompt_outproj(o_list, lse_list, conv_act, x_prompt, gt1, sc2, sh2, g_norm2[0], w_out_b,
                                           wr_hi, wr_lo, br)
    t = bp * s
    nblocks = (2 * t) // MOE_BLK + N_EXPERTS
    assert d == ROW_TILE * LANES
    dest, blk_e, nused, pend_ext = slot_plan(route, counts, nblocks)
    xs = moe_dispatch(h2.reshape(t, d), dest, pend_ext, nblocks * MOE_BLK)
    yb = moe_experts(xs, blk_e, nused, w_expert_gate[0], w_expert_up[0], w_expert_down[0])
    y_prompt = moe_combine(yb, dest, x1, route, gt2, g_final)

    xs2 = x_sample.reshape(ns, d)
    to_feature_major = lambda c: jnp.transpose(c, (0, 2, 3, 1)).reshape(c.shape[0], D_ATT, WIN_MAX)
    from_feature_major = lambda c: jnp.transpose(c.reshape(-1, N_HEADS, HEAD_DIM, WIN_MAX), (0, 3, 1, 2))[None]
    sq, sk, sv, sglu = sample_inproj(xs2, ssc1, ssh1, g_norm1[0], w_in_b)
    ko, vo, att_s = sample_cache_attention(sq, sk, sv, to_feature_major(cache_k[0]), to_feature_major(cache_v[0]),
                                           *sample_bias_tables(rel_bias))
    st_taps = jnp.transpose(state_conv[0], (1, 0, 2))
    y_s = sample_tail(att_s, sglu, st_taps, xs2, sgt1, ssc2, ssh2, sgt2, conv_w[0], conv_b[0], ln_g[0], ln_b[0],
                      w_out_b, g_norm2[0], wr_hi, wr_lo, br, g_final, w_expert_gate[0], w_expert_up[0],
                      w_expert_down[0])
    new_conv_s = jnp.transpose(jnp.concatenate([st_taps[1:], sglu[None]], axis=0), (1, 0, 2))

    return (y_prompt, y_s.reshape(ns, 1, d),
            from_feature_major(kc), from_feature_major(vc), tail[:, CONV_PAD - (CONV_W - 1):][None],
            from_feature_major(ko), from_feature_major(vo), new_conv_s[None])
```

```python
import functools
import math

import numpy as np
import jax
import jax.numpy as jnp
from jax import lax
from jax.experimental import pallas as pl
from jax.experimental.pallas import tpu as pltpu

F32 = jnp.float32
BF16 = jnp.bfloat16

D_MODEL = 1024
HEAD_DIM = 64
N_HEADS = 8
D_ATT = N_HEADS * HEAD_DIM
D_CONV = D_MODEL - D_ATT
DILATED_CFGS = ((128, 1), (512, 4), (2048, 16))
WIN_MAX = max(w for w, _ in DILATED_CFGS)
ATT_BLK = 128
CONV_W = 31
NUM_BUCKETS = 32
MAX_DISTANCE = WIN_MAX
N_GROUPS = 4
EXPERTS_PER_GROUP = 8
N_EXPERTS = N_GROUPS * EXPERTS_PER_GROUP
D_EXPERT = 512
EPS = 1e-6
NEG_INF = -1e30
PAST_LEN = 8192

LANES = 128
ROW_TILE = 8
CONV_PAD = 32
CONV_TAIL = 16
ROUTE_LANES = 128
GROUP_LANE0 = N_EXPERTS
MOE_BLK = 512
VMEM_LIMIT = 56 * 1024 * 1024


def _cparams(sem):
    return pltpu.CompilerParams(dimension_semantics=sem, vmem_limit_bytes=VMEM_LIMIT)


def _sigmoid(x):
    return 1.0 / (1.0 + jnp.exp(-x))


def _silu(x):
    return x * _sigmoid(x)


def _rms_scale(x):
    return x * lax.rsqrt(jnp.mean(x * x, axis=-1, keepdims=True) + EPS)


def _dot(a, b):
    return jnp.dot(a, b, preferred_element_type=F32)


def _ada_body(c_ref, w_ref, b_ref, o_ref):
    s = _silu(c_ref[...]).astype(BF16)
    o_ref[...] = _dot(s, w_ref[...].astype(BF16)) + b_ref[...]


def ada_modulation(c, w_ada, b_ada):
    rows, d = c.shape
    n = w_ada.shape[1]
    tn = 1024
    return pl.pallas_call(
        _ada_body,
        grid=(n // tn,),
        in_specs=[pl.BlockSpec((rows, d), lambda j: (0, 0)),
                  pl.BlockSpec((d, tn), lambda j: (0, j)),
                  pl.BlockSpec((1, tn), lambda j: (0, j))],
        out_specs=pl.BlockSpec((rows, tn), lambda j: (0, j)),
        out_shape=jax.ShapeDtypeStruct((rows, n), F32),
        compiler_params=_cparams(("arbitrary",)),
        name="ada_modulation",
    )(c, w_ada, b_ada.reshape(1, n))


def _conv_ln_swish(cbuf, tm, cw_ref, cb_ref, lg_ref, lb_ref, out_ref):
    chunk = 64
    base0 = CONV_PAD - (CONV_W - 1)
    for c in range(tm // chunk):
        pieces = []
        for lt in range(D_CONV // LANES):
            ls = slice(lt * LANES, (lt + 1) * LANES)
            acc = jnp.broadcast_to(cb_ref[:, ls], (chunk, LANES))
            for phase in range(ROW_TILE):
                part = None
                for j in range(CONV_W):
                    if (base0 + j) % ROW_TILE != phase:
                        continue
                    r0 = c * chunk + base0 + j - phase
                    term = cw_ref[j:j + 1, ls] * cbuf[r0:r0 + chunk + ROW_TILE, ls]
                    part = term if part is None else part + term
                if part is not None:
                    acc = acc + part[phase:phase + chunk, :]
            pieces.append(acc)
        acc = jnp.concatenate(pieces, axis=-1)
        mu = jnp.mean(acc, axis=-1, keepdims=True)
        xc = acc - mu
        var = jnp.mean(xc * xc, axis=-1, keepdims=True)
        y = xc * lax.rsqrt(var + EPS) * lg_ref[...] + lb_ref[...]
        out_ref[0, c * chunk:(c + 1) * chunk, :] = _silu(y).astype(out_ref.dtype)


def _lane_slabs_store(slab, rows):
    for c in range(rows.shape[1] // LANES):
        slab[c] = rows[:, c * LANES:(c + 1) * LANES]


def _residue_rows(slab, r, dilation):
    n = slab.shape[1] // dilation
    return jnp.concatenate([slab[c, pl.ds(r, n, stride=dilation), :] for c in range(slab.shape[0])], axis=-1)


def _store_residue_major(slab, rows, out_refs):
    _lane_slabs_store(slab, rows)
    for (_, dilation), ref in zip(DILATED_CFGS, out_refs):
        if dilation == 1:
            ref[0] = rows.astype(ref.dtype)
        else:
            for r in range(dilation):
                ref[0, r] = _residue_rows(slab, r, dilation).astype(ref.dtype)


def _inproj_body(x_ref, sc_ref, sh_ref, g_ref, w_ref, cw_ref, cb_ref, lg_ref, lb_ref,
                 q1, q4, q16, k1, k4, k16, v1, v4, v16, kc_ref, vc_ref, conv_ref, tail_ref, cbuf, slab,
                 *, first_cache_tile):
    i = pl.program_id(1)
    tm = x_ref.shape[1]

    @pl.when(i == 0)
    def _():
        cbuf[0:CONV_PAD, :] = jnp.zeros((CONV_PAD, D_CONV), F32)
        cbuf[CONV_PAD + tm:, :] = jnp.zeros((CONV_TAIL, D_CONV), F32)

    x = x_ref[0]
    h = (_rms_scale(x) * g_ref[...] * (1.0 + sc_ref[0]) + sh_ref[0]).astype(BF16)

    def col(j):
        return _dot(h, w_ref[:, j * D_ATT:(j + 1) * D_ATT])

    _store_residue_major(slab, col(0) * (HEAD_DIM ** -0.5), (q1, q4, q16))
    k = col(1)
    _store_residue_major(slab, k, (k1, k4, k16))
    v = col(2)
    _store_residue_major(slab, v, (v1, v4, v16))

    @pl.when(i >= first_cache_tile)
    def _():
        kc_ref[0] = k.T
        vc_ref[0] = v.T

    glu = col(3) * _sigmoid(col(4))
    cbuf[CONV_PAD:CONV_PAD + tm, :] = glu
    tail_ref[0] = glu[tm - CONV_PAD:, :]
    _conv_ln_swish(cbuf, tm, cw_ref, cb_ref, lg_ref, lb_ref, conv_ref)
    cbuf[0:CONV_PAD, :] = cbuf[tm:tm + CONV_PAD, :]


def prompt_inproj(x, sc1, sh1, g1, w_in_bf16, conv_w, conv_b, ln_g, ln_b, tm=512):
    b, s, d = x.shape
    nt = s // tm
    first_cache_tile = (s - WIN_MAX) // tm
    row = lambda v: v.reshape(1, -1)
    tok = lambda bi, i: (bi, i, 0)
    per_b = lambda bi, i: (bi, 0, 0)
    cst = lambda bi, i: (0, 0)
    cache = lambda bi, i: (bi, 0, jnp.maximum(i - first_cache_tile, 0))
    qkv_specs, qkv_shapes = [], []
    for _ in range(3):
        for _, dil in DILATED_CFGS:
            if dil == 1:
                qkv_specs.append(pl.BlockSpec((1, tm, D_ATT), tok))
                qkv_shapes.append(jax.ShapeDtypeStruct((b, s, D_ATT), BF16))
            else:
                qkv_specs.append(pl.BlockSpec((1, dil, tm // dil, D_ATT), lambda bi, i: (bi, 0, i, 0)))
                qkv_shapes.append(jax.ShapeDtypeStruct((b, dil, s // dil, D_ATT), BF16))
    outs = pl.pallas_call(
        functools.partial(_inproj_body, first_cache_tile=first_cache_tile),
        grid=(b, nt),
        in_specs=[pl.BlockSpec((1, tm, d), tok),
                  pl.BlockSpec((1, 1, d), per_b),
                  pl.BlockSpec((1, 1, d), per_b),
                  pl.BlockSpec((1, d), cst),
                  pl.BlockSpec(w_in_bf16.shape, cst),
                  pl.BlockSpec(conv_w.shape, cst),
                  pl.BlockSpec((1, D_CONV), cst),
                  pl.BlockSpec((1, D_CONV), cst),
                  pl.BlockSpec((1, D_CONV), cst)],
        out_specs=qkv_specs + [pl.BlockSpec((1, D_ATT, tm), cache),
                               pl.BlockSpec((1, D_ATT, tm), cache),
                               pl.BlockSpec((1, tm, D_CONV), tok),
                               pl.BlockSpec((1, CONV_PAD, D_CONV), per_b)],
        out_shape=qkv_shapes + [jax.ShapeDtypeStruct((b, D_ATT, WIN_MAX), F32),
                                jax.ShapeDtypeStruct((b, D_ATT, WIN_MAX), F32),
                                jax.ShapeDtypeStruct((b, s, D_CONV), BF16),
                                jax.ShapeDtypeStruct((b, CONV_PAD, D_CONV), F32)],
        scratch_shapes=[pltpu.VMEM((tm + CONV_PAD + CONV_TAIL, D_CONV), F32),
                        pltpu.VMEM((D_ATT // LANES, tm, LANES), F32)],
        compiler_params=_cparams(("arbitrary", "arbitrary")),
        name="prompt_inproj_conv",
    )(x, sc1, sh1, row(g1), w_in_bf16, conv_w, row(conv_b), row(ln_g), row(ln_b))
    nd = len(DILATED_CFGS)
    return outs[0:nd], outs[nd:2 * nd], outs[2 * nd:3 * nd], outs[3 * nd:]


def _attn_body(q_ref, kc_ref, kp_ref, vc_ref, vp_ref, bias_ref, hmask_ref, o_ref, lse_ref, *, wsub):
    i = pl.program_id(2)
    nq = q_ref.shape[0] // ATT_BLK
    npair = N_HEADS // 2
    qi = lax.broadcasted_iota(jnp.int32, (ATT_BLK, 2 * ATT_BLK), 0)
    kj = lax.broadcasted_iota(jnp.int32, (ATT_BLK, 2 * ATT_BLK), 1)
    rel = qi + ATT_BLK - kj
    band = (rel >= 0) & (rel <= wsub)
    band_first = band & ((kj >= ATT_BLK) | (i > 0))
    low = lax.broadcasted_iota(jnp.int32, (ATT_BLK, LANES), 1) < HEAD_DIM
    for sub in range(nq):
        r0 = sub * ATT_BLK
        valid = band_first if sub == 0 else band
        scores, values = [], []
        for hp in range(npair):
            cs = slice(hp * LANES, (hp + 1) * LANES)
            qp = q_ref[r0:r0 + ATT_BLK, cs]
            if sub == 0:
                kk = jnp.concatenate([kp_ref[:, cs], kc_ref[0:ATT_BLK, cs]], axis=0)
                vv = jnp.concatenate([vp_ref[:, cs], vc_ref[0:ATT_BLK, cs]], axis=0)
            else:
                kk = kc_ref[r0 - ATT_BLK:r0 + ATT_BLK, cs]
                vv = vc_ref[r0 - ATT_BLK:r0 + ATT_BLK, cs]
            values.append(vv)
            for half in range(2):
                s = lax.dot_general(qp * hmask_ref[half], kk, (((1,), (1,)), ((), ())), preferred_element_type=F32)
                scores.append(jnp.where(valid, s + bias_ref[2 * hp + half], NEG_INF))
        s = jnp.concatenate(scores, axis=0)
        m = jnp.max(s, axis=-1, keepdims=True)
        p = jnp.exp(s - m)
        l = jnp.sum(p, axis=-1, keepdims=True)
        p16 = p.astype(BF16)
        inv = 1.0 / l
        lse = m + jnp.log(l)
        for hp in range(npair):
            cs = slice(hp * LANES, (hp + 1) * LANES)
            ra = slice(2 * hp * ATT_BLK, (2 * hp + 1) * ATT_BLK)
            rb = slice((2 * hp + 1) * ATT_BLK, (2 * hp + 2) * ATT_BLK)
            oa = _dot(p16[ra], values[hp]) * inv[ra]
            ob = _dot(p16[rb], values[hp]) * inv[rb]
            o_ref[r0:r0 + ATT_BLK, cs] = jnp.where(low, oa, ob).astype(o_ref.dtype)
            lse_ref[r0:r0 + ATT_BLK, cs] = jnp.where(low, lse[ra], lse[rb])


def attention_branch(q, k, v, bias, window, dilation, qblocks=4):
    c = q.shape[-1]
    b, n = q.shape[0], q.shape[-2]
    wsub = window // dilation
    tq = ATT_BLK * qblocks
    if dilation == 1:
        cur_spec = pl.BlockSpec((None, tq, c), lambda bi, r, i: (bi, i, 0))
        prev_spec = pl.BlockSpec((None, ATT_BLK, c), lambda bi, r, i: (bi, jnp.maximum(i * qblocks - 1, 0), 0))
    else:
        cur_spec = pl.BlockSpec((None, None, tq, c), lambda bi, r, i: (bi, r, i, 0))
        prev_spec = pl.BlockSpec((None, None, ATT_BLK, c),
                                 lambda bi, r, i: (bi, r, jnp.maximum(i * qblocks - 1, 0), 0))
    lane = jnp.arange(LANES)[None, :]
    hmask = jnp.broadcast_to(jnp.stack([lane < HEAD_DIM, lane >= HEAD_DIM]).astype(BF16), (2, ATT_BLK, LANES))
    return pl.pallas_call(
        functools.partial(_attn_body, wsub=wsub),
        grid=(b, dilation, n // tq),
        in_specs=[cur_spec, cur_spec, prev_spec, cur_spec, prev_spec,
                  pl.BlockSpec(bias.shape, lambda bi, r, i: (0, 0, 0)),
                  pl.BlockSpec(hmask.shape, lambda bi, r, i: (0, 0, 0))],
        out_specs=[cur_spec, cur_spec],
        out_shape=[jax.ShapeDtypeStruct(q.shape, BF16), jax.ShapeDtypeStruct(q.shape, F32)],
        compiler_params=_cparams(("arbitrary", "arbitrary", "arbitrary")),
        name=f"attn_branch_d{dilation}",
    )(q, k, k, v, v, bias, hmask)


def _split_bf16(a):
    hi = a.astype(BF16)
    lo = (a - hi.astype(F32)).astype(BF16)
    return hi, lo


def _router_logits(h2, wr_hi_ref, wr_lo_ref, br_ref):
    hi, lo = _split_bf16(h2)
    return _dot(hi, wr_hi_ref[...]) + (_dot(hi, wr_lo_ref[...]) + _dot(lo, wr_hi_ref[...])) + br_ref[...]


def _route(logits):
    rows = logits.shape[0]
    lane = lax.broadcasted_iota(jnp.int32, (rows, ROUTE_LANES), 1)
    lanef = lane.astype(F32)
    big = float(ROUTE_LANES)
    is_g = (lane >= GROUP_LANE0) & (lane < GROUP_LANE0 + N_GROUPS)
    glog = jnp.where(is_g, logits, -jnp.inf)
    gmax = jnp.max(glog, axis=-1, keepdims=True)
    gsum = jnp.sum(jnp.exp(glog - gmax), axis=-1, keepdims=True)
    p_grp = 1.0 / gsum
    glane = jnp.min(jnp.where(glog == gmax, lanef, big), axis=-1, keepdims=True)
    grp = glane - float(GROUP_LANE0)
    in_grp = jnp.floor(lanef * (1.0 / EXPERTS_PER_GROUP)) == grp
    elog = jnp.where(in_grp, logits, -jnp.inf)
    emax = jnp.max(elog, axis=-1, keepdims=True)
    eexp = jnp.exp(elog - emax)
    pe = eexp / jnp.sum(eexp, axis=-1, keepdims=True)
    pe = jnp.where(in_grp, pe, -1.0)
    p1 = jnp.max(pe, axis=-1, keepdims=True)
    e1 = jnp.min(jnp.where(pe == p1, lanef, big), axis=-1, keepdims=True)
    pe2 = jnp.where(lanef == e1, -1.0, pe)
    p2 = jnp.max(pe2, axis=-1, keepdims=True)
    e2 = jnp.min(jnp.where(pe2 == p2, lanef, big), axis=-1, keepdims=True)
    psum = p1 + p2
    return e1, e2, p_grp * p1 / psum, p_grp * p2 / psum


def _position_order(ref, slab, dilation):
    if dilation == 1:
        return ref[0].astype(F32)
    n = ref.shape[2]
    for r in range(dilation):
        piece = ref[0, r].astype(F32)
        for c in range(slab.shape[0]):
            slab[c, pl.ds(r, n, stride=dilation), :] = piece[:, c * LANES:(c + 1) * LANES]
    return jnp.concatenate([slab[c] for c in range(slab.shape[0])], axis=-1)


def _mix_branches(o_refs, lse_refs, slabs):
    slabs = iter(slabs)
    lses = [_position_order(r, None if d == 1 else next(slabs), d) for r, (_, d) in zip(lse_refs, DILATED_CFGS)]
    outs = [_position_order(r, None if d == 1 else next(slabs), d) for r, (_, d) in zip(o_refs, DILATED_CFGS)]
    mx = jnp.maximum(jnp.maximum(lses[0], lses[1]), lses[2])
    ws = [jnp.exp(l - mx) for l in lses]
    numer = sum(w * o for w, o in zip(ws, outs))
    return numer / (ws[0] + ws[1] + ws[2])


def _outproj_body(o1, o4, o16, l1, l4, l16, conv_ref, x_ref, gt_ref, sc_ref, sh_ref, g_ref,
                  wo_ref, wrh_ref, wrl_ref, br_ref,
                  x1_ref, h2_ref, route_ref, cnt_ref, carry, *slabs):
    first = (pl.program_id(0) == 0) & (pl.program_id(1) == 0)

    @pl.when(first)
    def _():
        carry[...] = jnp.zeros_like(carry)

    tm = x_ref.shape[1]
    att = _mix_branches((o1, o4, o16), (l1, l4, l16), slabs).astype(BF16)
    mix = _dot(att, wo_ref[0:D_ATT, :]) + _dot(conv_ref[0], wo_ref[D_ATT:, :])
    x1 = x_ref[0] + gt_ref[0] * mix
    x1_ref[0] = x1
    h2 = _rms_scale(x1) * g_ref[...] * (1.0 + sc_ref[0]) + sh_ref[0]
    h2_ref[0] = h2.astype(BF16)
    e1, e2, w1, w2 = _route(_router_logits(h2, wrh_ref, wrl_ref, br_ref))

    lane = lax.broadcasted_iota(jnp.int32, (tm, ROUTE_LANES), 1).astype(F32)
    oh1 = lane == e1
    oh2 = lane == e2
    onehot = jnp.where(oh1 | oh2, 1.0, 0.0)
    ri = lax.broadcasted_iota(jnp.int32, (tm, tm), 0)
    ci = lax.broadcasted_iota(jnp.int32, (tm, tm), 1)
    tri = jnp.where(ci < ri, 1.0, 0.0).astype(BF16)
    before = _dot(tri, onehot.astype(BF16)) + carry[...]
    r1 = jnp.sum(jnp.where(oh1, before, 0.0), axis=-1, keepdims=True)
    r2 = jnp.sum(jnp.where(oh2, before, 0.0), axis=-1, keepdims=True)
    carry[...] = carry[...] + jnp.sum(onehot, axis=0, keepdims=True)
    cnt_ref[...] = carry[...]
    rec = jnp.where(lane == 0.0, e1, 0.0)
    for idx, val in ((1, e2), (2, w1), (3, w2), (4, r1), (5, r2)):
        rec = jnp.where(lane == float(idx), val, rec)
    route_ref[0] = rec


def prompt_outproj(o_list, lse_list, conv_act, x, gt1, sc2, sh2, g2, w_out_bf16, wr_hi, wr_lo, br, tm=512):
    b, s, d = x.shape
    tok = lambda bi, i: (bi, i, 0)
    per_b = lambda bi, i: (bi, 0, 0)
    cst = lambda bi, i: (0, 0)
    half = pl.BlockSpec((1, tm, D_ATT), tok)
    branch_specs = [half if dil == 1 else pl.BlockSpec((1, dil, tm // dil, D_ATT), lambda bi, i: (bi, 0, i, 0))
                    for _, dil in DILATED_CFGS]
    n_slabs = 2 * sum(1 for _, dil in DILATED_CFGS if dil > 1)
    return pl.pallas_call(
        _outproj_body,
        grid=(b, s // tm),
        in_specs=branch_specs * 2 + [half, pl.BlockSpec((1, tm, d), tok),
                               pl.BlockSpec((1, 1, d), per_b),
                               pl.BlockSpec((1, 1, d), per_b),
                               pl.BlockSpec((1, 1, d), per_b),
                               pl.BlockSpec((1, d), cst),
                               pl.BlockSpec(w_out_bf16.shape, cst),
                               pl.BlockSpec(wr_hi.shape, cst),
                               pl.BlockSpec(wr_lo.shape, cst),
                               pl.BlockSpec((1, ROUTE_LANES), cst)],
        out_specs=[pl.BlockSpec((1, tm, d), tok),
                   pl.BlockSpec((1, tm, d), tok),
                   pl.BlockSpec((1, tm, ROUTE_LANES), tok),
                   pl.BlockSpec((1, ROUTE_LANES), cst)],
        out_shape=[jax.ShapeDtypeStruct((b, s, d), F32),
                   jax.ShapeDtypeStruct((b, s, d), BF16),
                   jax.ShapeDtypeStruct((b, s, ROUTE_LANES), F32),
                   jax.ShapeDtypeStruct((1, ROUTE_LANES), F32)],
        scratch_shapes=[pltpu.VMEM((1, ROUTE_LANES), F32)] + [pltpu.VMEM((D_ATT // LANES, tm, LANES), F32)] * n_slabs,
        compiler_params=_cparams(("arbitrary", "arbitrary")),
        name="prompt_outproj_router",
    )(*o_list, *lse_list, conv_act, x, gt1, sc2, sh2, g2.reshape(1, d), w_out_bf16, wr_hi, wr_lo, br)


def router_weights(w_rg, b_rg, w_re, b_re):
    d = w_rg.shape[0]
    fill = ROUTE_LANES - N_EXPERTS - N_GROUPS
    w = jnp.concatenate([w_re.astype(F32), w_rg.astype(F32), jnp.zeros((d, fill), F32)], axis=1)
    bias = jnp.concatenate([b_re.astype(F32), b_rg.astype(F32), jnp.zeros((fill,), F32)])[None, :]
    hi = w.astype(BF16)
    lo = (w - hi.astype(F32)).astype(BF16)
    return hi, lo, bias


def _rows_from_tiles(tile_ref, n):
    return jnp.concatenate([tile_ref[pl.ds(c, n, stride=ROW_TILE), :] for c in range(ROW_TILE)], axis=-1)


def _rows_to_tiles(tile_ref, rows):
    n = rows.shape[0]
    for c in range(ROW_TILE):
        tile_ref[pl.ds(c, n, stride=ROW_TILE), :] = rows[:, c * LANES:(c + 1) * LANES]


def _dispatch_body(dest_ref, pend_ref, h2_ref, xs_hbm, tiles, zbuf, sem, zsem, *, chunk):
    step = pl.program_id(0)

    @pl.when(step == 0)
    def _():
        zbuf[...] = jnp.zeros_like(zbuf)

        def zcopy(e):
            start = pl.multiple_of(jnp.maximum(pend_ref[e + 1] - MOE_BLK * ROW_TILE, 0), ROW_TILE)
            return pltpu.make_async_copy(zbuf, xs_hbm.at[pl.ds(start, MOE_BLK * ROW_TILE)], zsem)

        def zstart(e, c):
            @pl.when(pend_ref[e + 1] > pend_ref[e])
            def _():
                zcopy(e).start()
            return c

        def zwait(e, c):
            @pl.when(pend_ref[e + 1] > pend_ref[e])
            def _():
                zcopy(e).wait()
            return c

        lax.fori_loop(0, N_EXPERTS, zstart, 0)
        lax.fori_loop(0, N_EXPERTS, zwait, 0)

        blk_rows = MOE_BLK * ROW_TILE
        nblk = xs_hbm.shape[0] // blk_rows

        def tcopy(i):
            return pltpu.make_async_copy(zbuf, xs_hbm.at[pl.ds(pl.multiple_of(i * blk_rows, blk_rows), blk_rows)],
                                         zsem)

        def tstart(i, c):
            @pl.when(i * blk_rows >= pend_ref[N_EXPERTS])
            def _():
                tcopy(i).start()
            return c

        def twait(i, c):
            @pl.when(i * blk_rows >= pend_ref[N_EXPERTS])
            def _():
                tcopy(i).wait()
            return c

        lax.fori_loop(0, nblk, tstart, 0)
        lax.fori_loop(0, nblk, twait, 0)

    _rows_to_tiles(tiles, h2_ref[...].astype(F32))

    def copy(t, k):
        src = pl.multiple_of(t * ROW_TILE, ROW_TILE)
        dst = pl.multiple_of(dest_ref[0, 0, 2 * t + k], ROW_TILE)
        return pltpu.make_async_copy(tiles.at[pl.ds(src, ROW_TILE)], xs_hbm.at[pl.ds(dst, ROW_TILE)], sem)

    def start(t, c):
        copy(t, 0).start(priority=0)
        copy(t, 1).start(priority=1)
        return c

    def wait(t, c):
        copy(t, 0).wait()
        copy(t, 1).wait()
        return c

    lax.fori_loop(0, chunk, start, 0, unroll=8)
    lax.fori_loop(0, chunk, wait, 0, unroll=8)


def moe_dispatch(h2_flat, dest_rt, pends_rt, nslots, chunk=512):
    t, d = h2_flat.shape
    assert d == ROW_TILE * LANES
    nsteps = t // chunk
    dest3 = dest_rt.reshape(nsteps, 1, 2 * chunk)
    return pl.pallas_call(
        functools.partial(_dispatch_body, chunk=chunk),
        grid=(nsteps,),
        in_specs=[pl.BlockSpec((1, 1, 2 * chunk), lambda i: (i, 0, 0), memory_space=pltpu.SMEM),
                  pl.BlockSpec(memory_space=pltpu.SMEM),
                  pl.BlockSpec((chunk, d), lambda i: (i, 0))],
        out_specs=pl.BlockSpec(memory_space=pl.ANY),
        out_shape=jax.ShapeDtypeStruct((nslots * ROW_TILE, LANES), F32),
        scratch_shapes=[pltpu.VMEM((chunk * ROW_TILE, LANES), F32),
                        pltpu.VMEM((MOE_BLK * ROW_TILE, LANES), F32),
                        pltpu.SemaphoreType.DMA(()),
                        pltpu.SemaphoreType.DMA(())],
        compiler_params=_cparams(("arbitrary",)),
        name="moe_dispatch",
    )(dest3, pends_rt, h2_flat)


def _experts_body(blk_e_ref, nused_ref, xs_ref, w1_ref, w3_ref, w2_ref, y_ref, w1b, w3b, w2b):
    i = pl.program_id(0)
    new_expert = (i == 0) | (blk_e_ref[i] != blk_e_ref[jnp.maximum(i - 1, 0)])

    @pl.when((i < nused_ref[0]) & new_expert)
    def _():
        w1b[...] = w1_ref[0].astype(BF16)
        w3b[...] = w3_ref[0].astype(BF16)
        w2b[...] = w2_ref[0].astype(BF16)

    @pl.when(i < nused_ref[0])
    def _():
        xb = _rows_from_tiles(xs_ref, MOE_BLK).astype(BF16)
        a = (_silu(_dot(xb, w1b[...])) * _dot(xb, w3b[...])).astype(BF16)
        _rows_to_tiles(y_ref, _dot(a, w2b[...]))

    @pl.when(i >= nused_ref[0])
    def _():
        y_ref[...] = jnp.zeros_like(y_ref)


def moe_experts(xs, blk_e, nused, w_e1, w_e3, w_e2):
    nb = xs.shape[0] // (MOE_BLK * ROW_TILE)
    _, d, de = w_e1.shape
    blk = lambda i, be, nu: (jnp.minimum(i, nu[0] - 1), 0)
    wsel = lambda i, be, nu: (be[i], 0, 0)
    grid_spec = pltpu.PrefetchScalarGridSpec(
        num_scalar_prefetch=2,
        grid=(nb,),
        in_specs=[pl.BlockSpec((MOE_BLK * ROW_TILE, LANES), blk),
                  pl.BlockSpec((1, d, de), wsel),
                  pl.BlockSpec((1, d, de), wsel),
                  pl.BlockSpec((1, de, d), wsel)],
        out_specs=pl.BlockSpec((MOE_BLK * ROW_TILE, LANES), lambda i, be, nu: (i, 0)),
        scratch_shapes=[pltpu.VMEM((d, de), BF16), pltpu.VMEM((d, de), BF16), pltpu.VMEM((de, d), BF16)],
    )
    return pl.pallas_call(
        _experts_body,
        grid_spec=grid_spec,
        out_shape=jax.ShapeDtypeStruct(xs.shape, F32),
        compiler_params=_cparams(("arbitrary",)),
        name="moe_experts",
    )(blk_e, nused, xs, w_e1, w_e3, w_e2)


def _combine_body(dest_ref, yb_hbm, x1_ref, route_ref, gt_ref, gf_ref, out_ref, ya0, ya1, sem, *, tm):
    def copy(t, k, buf):
        src = pl.multiple_of(dest_ref[0, 0, 2 * t + k], ROW_TILE)
        dst = pl.multiple_of(t * ROW_TILE, ROW_TILE)
        return pltpu.make_async_copy(yb_hbm.at[pl.ds(src, ROW_TILE)], buf.at[pl.ds(dst, ROW_TILE)], sem)

    def start(t, c):
        copy(t, 0, ya0).start(priority=0)
        copy(t, 1, ya1).start(priority=1)
        return c

    def wait(t, c):
        copy(t, 0, ya0).wait()
        copy(t, 1, ya1).wait()
        return c

    lax.fori_loop(0, tm, start, 0, unroll=8)
    lax.fori_loop(0, tm, wait, 0, unroll=8)
    rec = route_ref[0]
    moe = rec[:, 2:3] * _rows_from_tiles(ya0, tm) + rec[:, 3:4] * _rows_from_tiles(ya1, tm)
    y = x1_ref[0] + gt_ref[0] * moe
    out_ref[0] = _rms_scale(y) * gf_ref[...]


def moe_combine(yb, dest_rt, x1, route, gt2, g_final, tm=256):
    b, s, d = x1.shape
    nt = s // tm
    dest3 = dest_rt.reshape(b * nt, 1, 2 * tm)
    tok = lambda bi, i: (bi, i, 0)
    return pl.pallas_call(
        functools.partial(_combine_body, tm=tm),
        grid=(b, nt),
        in_specs=[pl.BlockSpec((1, 1, 2 * tm), lambda bi, i: (bi * nt + i, 0, 0), memory_space=pltpu.SMEM),
                  pl.BlockSpec(memory_space=pl.ANY),
                  pl.BlockSpec((1, tm, d), tok),
                  pl.BlockSpec((1, tm, ROUTE_LANES), tok),
                  pl.BlockSpec((1, 1, d), lambda bi, i: (bi, 0, 0)),
                  pl.BlockSpec((1, d), lambda bi, i: (0, 0))],
        out_specs=pl.BlockSpec((1, tm, d), tok),
        out_shape=jax.ShapeDtypeStruct((b, s, d), F32),
        scratch_shapes=[pltpu.VMEM((tm * ROW_TILE, LANES), F32), pltpu.VMEM((tm * ROW_TILE, LANES), F32),
                        pltpu.SemaphoreType.DMA(())],
        compiler_params=_cparams(("arbitrary", "arbitrary")),
        name="moe_combine_final",
    )(dest3, yb, x1, route, gt2, g_final.reshape(1, d))


def slot_plan(route, counts, nblocks):
    cnt = counts[0, :N_EXPERTS].astype(jnp.int32)
    padded = (cnt + MOE_BLK - 1) // MOE_BLK * MOE_BLK
    pends = jnp.cumsum(padded)
    pstarts = pends - padded
    eid = route[..., 0:2].astype(jnp.int32)
    rank = route[..., 4:6].astype(jnp.int32)
    experts = jnp.arange(N_EXPERTS, dtype=jnp.int32)
    start_of = jnp.sum(jnp.where(eid[..., None] == experts, pstarts, 0), axis=-1)
    dest = (start_of + rank).reshape(-1)
    blk_start = jnp.arange(nblocks, dtype=jnp.int32) * MOE_BLK
    blk_e = jnp.minimum(jnp.sum((pends[None, :] <= blk_start[:, None]).astype(jnp.int32), axis=1), N_EXPERTS - 1)
    nused = (pends[-1:] // MOE_BLK).astype(jnp.int32)
    pend_ext = jnp.concatenate([jnp.zeros((1,), jnp.int32), pends.astype(jnp.int32)])
    return (dest * ROW_TILE).astype(jnp.int32), blk_e, nused, (pend_ext * ROW_TILE).astype(jnp.int32)


def _sample_inproj_body(x_ref, sc_ref, sh_ref, g_ref, w_ref, q_ref, k_ref, v_ref, glu_ref):
    h = (_rms_scale(x_ref[...]) * g_ref[...] * (1.0 + sc_ref[...]) + sh_ref[...]).astype(BF16)

    def col(j):
        return _dot(h, w_ref[:, j * D_ATT:(j + 1) * D_ATT])

    q_ref[...] = col(0) * (HEAD_DIM ** -0.5)
    k_ref[...] = col(1)
    v_ref[...] = col(2)
    glu_ref[...] = col(3) * _sigmoid(col(4))


def sample_inproj(x, sc1, sh1, g1, w_in_bf16):
    n, d = x.shape
    out = jax.ShapeDtypeStruct((n, D_ATT), F32)
    return pl.pallas_call(
        _sample_inproj_body,
        out_shape=[out, out, out, out],
        compiler_params=pltpu.CompilerParams(vmem_limit_bytes=VMEM_LIMIT),
        name="sample_inproj",
    )(x, sc1, sh1, g1.reshape(1, d), w_in_bf16)


def _sample_cache_body(qt_ref, knt_ref, vnt_ref, k_ref, v_ref, bias_ref, valid_ref, bias0_ref, valid0_ref,
                       ko_ref, vo_ref, att_ref, s_s, p_s, n_s):
    b = pl.program_id(0)
    win = k_ref.shape[2]
    sel = lax.broadcasted_iota(jnp.int32, qt_ref.shape, 1) == b

    def column(ref):
        return jnp.sum(jnp.where(sel, ref[...], 0.0), axis=1, keepdims=True)

    q, kn, vn = column(qt_ref), column(knt_ref), column(vnt_ref)
    kt = k_ref[0]
    vt = v_ref[0]
    for h in range(N_HEADS):
        rows = slice(h * HEAD_DIM, (h + 1) * HEAD_DIM)
        s_s[h:h + 1, :] = jnp.sum(kt[rows, :] * q[rows, :], axis=0, keepdims=True)
        n_s[h:h + 1, :] = jnp.broadcast_to(jnp.sum(kn[rows, :] * q[rows, :], axis=0, keepdims=True), (1, LANES))
    s_all = s_s[...]
    s_new = n_s[...][:, 0:1] + bias0_ref[:, 0:1]
    parts = []
    for br in range(len(DILATED_CFGS)):
        s = jnp.where(valid_ref[br] > 0.0, s_all + bias_ref[br], NEG_INF)
        s0 = jnp.where(valid0_ref[br][:, 0:1] > 0.0, s_new, NEG_INF)
        m = jnp.maximum(jnp.max(s, axis=-1, keepdims=True), s0)
        p = jnp.exp(s - m)
        p0 = jnp.exp(s0 - m)
        parts.append((p, p0, m, jnp.sum(p, axis=-1, keepdims=True) + p0))
    mx = jnp.maximum(jnp.maximum(parts[0][2], parts[1][2]), parts[2][2])
    ws = [jnp.exp(m - mx) for _, _, m, _ in parts]
    inv = 1.0 / sum(w * l for w, (_, _, _, l) in zip(ws, parts))
    p_s[...] = sum(w * p for w, (p, _, _, _) in zip(ws, parts)) * inv
    p_new = sum(w * p0 for w, (_, p0, _, _) in zip(ws, parts)) * inv
    cols = []
    for h in range(N_HEADS):
        rows = slice(h * HEAD_DIM, (h + 1) * HEAD_DIM)
        cols.append(jnp.sum(vt[rows, :] * p_s[h:h + 1, :], axis=1, keepdims=True) + p_new[h:h + 1, :] * vn[rows, :])
    att = jnp.concatenate(cols, axis=0)

    @pl.when(b == 0)
    def _():
        att_ref[...] = jnp.zeros_like(att_ref)

    att_ref[...] = jnp.where(sel, att, att_ref[...])
    last = lax.broadcasted_iota(jnp.int32, kt.shape, 1) == win - 1
    ko_ref[0] = jnp.where(last, kn, pltpu.roll(kt, win - 1, 1))
    vo_ref[0] = jnp.where(last, vn, pltpu.roll(vt, win - 1, 1))


def sample_cache_attention(q, k_new, v_new, cache_kt, cache_vt, bias, valid, bias0, valid0):
    n, c, win = cache_kt.shape
    full = lambda a: pl.BlockSpec(a.shape, lambda i: (0,) * a.ndim)
    seq = pl.BlockSpec((1, c, win), lambda i: (i, 0, 0))
    qt, knt, vnt = q.T, k_new.T, v_new.T
    ko, vo, att_t = pl.pallas_call(
        _sample_cache_body,
        grid=(n,),
        in_specs=[full(qt), full(knt), full(vnt), seq, seq, full(bias), full(valid), full(bias0), full(valid0)],
        out_specs=[seq, seq, pl.BlockSpec((c, n), lambda i: (0, 0))],
        out_shape=[jax.ShapeDtypeStruct(cache_kt.shape, F32), jax.ShapeDtypeStruct(cache_vt.shape, F32),
                   jax.ShapeDtypeStruct((c, n), F32)],
        scratch_shapes=[pltpu.VMEM((N_HEADS, win), F32), pltpu.VMEM((N_HEADS, win), F32),
                        pltpu.VMEM((N_HEADS, LANES), F32)],
        compiler_params=_cparams(("arbitrary",)),
        name="sample_cache_attention",
    )(qt, knt, vnt, cache_kt, cache_vt, bias, valid, bias0, valid0)
    return ko, vo, att_t.T


def _sample_tail_body(att_ref, glu_ref, st_ref, x_ref, gt1_ref, sc2_ref, sh2_ref, gt2_ref,
                      cw_ref, cb_ref, lg_ref, lb_ref, wo_ref, g2_ref, wrh_ref, wrl_ref, br_ref, gf_ref,
                      w1_ref, w3_ref, w2_ref, y_ref, x1_s, h2_s, gate_s, acc_s):
    e = pl.program_id(0)
    n = x_ref.shape[0]

    @pl.when(e == 0)
    def _():
        nhist = CONV_W - 1
        conv = cw_ref[nhist:CONV_W, :] * glu_ref[...] + cb_ref[...]
        for j in range(nhist):
            conv = conv + cw_ref[j:j + 1, :] * st_ref[j]
        mu = jnp.mean(conv, axis=-1, keepdims=True)
        xc = conv - mu
        var = jnp.mean(xc * xc, axis=-1, keepdims=True)
        act = _silu(xc * lax.rsqrt(var + EPS) * lg_ref[...] + lb_ref[...]).astype(BF16)
        mix = _dot(att_ref[...].astype(BF16), wo_ref[0:D_ATT, :]) + _dot(act, wo_ref[D_ATT:, :])
        x1 = x_ref[...] + gt1_ref[...] * mix
        x1_s[...] = x1
        h2 = _rms_scale(x1) * g2_ref[...] * (1.0 + sc2_ref[...]) + sh2_ref[...]
        h2_s[...] = h2.astype(BF16)
        e1, e2, w1, w2 = _route(_router_logits(h2, wrh_ref, wrl_ref, br_ref))
        lane = lax.broadcasted_iota(jnp.int32, (n, ROUTE_LANES), 1).astype(F32)
        gate_s[...] = jnp.where(lane == e1, w1, 0.0) + jnp.where(lane == e2, w2, 0.0)
        acc_s[...] = jnp.zeros_like(acc_s)

    hb = h2_s[...]
    a = (_silu(_dot(hb, w1_ref[0].astype(BF16))) * _dot(hb, w3_ref[0].astype(BF16))).astype(BF16)
    ye = _dot(a, w2_ref[0].astype(BF16))
    lane = lax.broadcasted_iota(jnp.int32, (n, ROUTE_LANES), 1)
    g = jnp.sum(jnp.where(lane == e, gate_s[...], 0.0), axis=-1, keepdims=True)
    acc_s[...] = acc_s[...] + g * ye

    @pl.when(e == pl.num_programs(0) - 1)
    def _():
        y = x1_s[...] + gt2_ref[...] * acc_s[...]
        y_ref[...] = _rms_scale(y) * gf_ref[...]


def sample_tail(att, glu, state_conv, x, gt1, sc2, sh2, gt2, conv_w, conv_b, ln_g, ln_b, w_out_bf16, g2,
                wr_hi, wr_lo, br, g_final, w_e1, w_e3, w_e2):
    n, d = x.shape
    de = w_e1.shape[2]
    row = lambda v: v.reshape(1, -1)
    full = lambda a: pl.BlockSpec(a.shape, lambda e: (0,) * a.ndim)
    args = [att, glu, state_conv, x, gt1, sc2, sh2, gt2, conv_w, row(conv_b), row(ln_g), row(ln_b), w_out_bf16,
            row(g2), wr_hi, wr_lo, br, row(g_final)]
    return pl.pallas_call(
        _sample_tail_body,
        grid=(N_EXPERTS,),
        in_specs=[full(a) for a in args] + [pl.BlockSpec((1, d, de), lambda e: (e, 0, 0)),
                                            pl.BlockSpec((1, d, de), lambda e: (e, 0, 0)),
                                            pl.BlockSpec((1, de, d), lambda e: (e, 0, 0))],
        out_specs=pl.BlockSpec((n, d), lambda e: (0, 0)),
        out_shape=jax.ShapeDtypeStruct((n, d), F32),
        scratch_shapes=[pltpu.VMEM((n, d), F32), pltpu.VMEM((n, d), BF16),
                        pltpu.VMEM((n, ROUTE_LANES), F32), pltpu.VMEM((n, d), F32)],
        compiler_params=_cparams(("arbitrary",)),
        name="sample_tail_moe",
    )(*args, w_e1, w_e3, w_e2)


def _t5_bucket_static(dist):
    n = np.maximum(np.asarray(dist, np.int64), 0)
    max_exact = NUM_BUCKETS // 2
    x = np.log(np.maximum(n, 1) / max_exact) / math.log(MAX_DISTANCE / max_exact) * (NUM_BUCKETS - max_exact)
    on_edge = (np.abs(x - np.rint(x)) < 2e-5) & (x > 0.5) & (np.rint(x) < NUM_BUCKETS - max_exact)
    assert not on_edge.any()
    large = np.minimum(max_exact + np.floor(x + 1e-9).astype(np.int64), NUM_BUCKETS - 1)
    return np.where(n < max_exact, n, large)


def _bias_lookup(rel_bias, dist):
    bucket = _t5_bucket_static(dist)
    onehot = (bucket.reshape(-1, 1) == np.arange(NUM_BUCKETS)[None, :]).astype(np.float32)
    table = jnp.einsum('nb,bh->hn', onehot, rel_bias.astype(F32), precision=lax.Precision.HIGHEST)
    return table.reshape((rel_bias.shape[1],) + bucket.shape)


def prompt_bias_table(rel_bias, dilation):
    i = np.arange(ATT_BLK)[:, None]
    j = np.arange(2 * ATT_BLK)[None, :]
    return _bias_lookup(rel_bias, (i + ATT_BLK - j) * dilation)


def sample_bias_tables(rel_bias):
    dist = WIN_MAX - np.arange(WIN_MAX)
    slot_bias = _bias_lookup(rel_bias, dist)
    bias, valid, valid0 = [], [], []
    for window, d in DILATED_CFGS:
        member = (dist % d == 0) & (dist <= window) & ((PAST_LEN - dist) >= 0)
        bias.append(slot_bias)
        valid.append(jnp.asarray(np.broadcast_to(member.astype(np.float32)[None, :], (N_HEADS, WIN_MAX))))
        valid0.append(jnp.full((N_HEADS, LANES), float(PAST_LEN >= 0), F32))
    bias0 = jnp.broadcast_to(_bias_lookup(rel_bias, np.zeros((1,), np.int64)), (N_HEADS, LANES))
    return jnp.stack(bias), jnp.stack(valid), bias0, jnp.stack(valid0)


def kernel(x_prompt, x_sample, cache_k, cache_v, state_conv, c_prompt, c_sample, rel_bias, w_ada, b_ada, g_norm1,
           w_in, conv_w, conv_b, ln_g, ln_b, w_out, g_norm2, w_router_group, b_router_group, w_router_expert,
           b_router_expert, w_expert_gate, w_expert_up, w_expert_down, g_final):
    depth = w_ada.shape[0]
    assert depth == 1, "single-layer step"
    bp, s, d = x_prompt.shape
    ns = x_sample.shape[0]
    w_in_b = w_in[0].astype(BF16)
    w_out_b = w_out[0].astype(BF16)
    wr_hi, wr_lo, br = router_weights(w_router_group[0], b_router_group[0], w_router_expert[0], b_router_expert[0])

    pad = (-(bp + ns)) % 8
    c_all = jnp.concatenate([c_prompt, c_sample, jnp.zeros((pad, d), c_prompt.dtype)], axis=0)
    mod = ada_modulation(c_all, w_ada[0], b_ada[0])
    mod_p = mod[:bp].reshape(bp, 1, 6 * d)
    sh1, sc1, gt1, sh2, sc2, gt2 = [mod_p[..., j * d:(j + 1) * d] for j in range(6)]
    mod_s = mod[bp:bp + ns]
    ssh1, ssc1, sgt1, ssh2, ssc2, sgt2 = [mod_s[:, j * d:(j + 1) * d] for j in range(6)]

    qs, ks, vs, (kc, vc, conv_act, tail) = prompt_inproj(x_prompt, sc1, sh1, g_norm1[0], w_in_b, conv_w[0], conv_b[0],
                                                         ln_g[0], ln_b[0])
    o_list, lse_list = [], []
    for (window, dil), q, k, v in zip(DILATED_CFGS, qs, ks, vs):
        o, lse = attention_branch(q, k, v, prompt_bias_table(rel_bias, dil), window, dil)
        o_list.append(o)
        lse_list.append(lse)
    x1, h2, route, counts = prompt_outproj(o_list, lse_list, conv_act, x_prompt, gt1, sc2, sh2, g_norm2[0], w_out_b,
                                           wr_hi, wr_lo, br)
    t = bp * s
    nblocks = (2 * t) // MOE_BLK + N_EXPERTS
    assert d == ROW_TILE * LANES
    dest, blk_e, nused, pend_ext = slot_plan(route, counts, nblocks)
    xs = moe_dispatch(h2.reshape(t, d), dest, pend_ext, nblocks * MOE_BLK)
    yb = moe_experts(xs, blk_e, nused, w_expert_gate[0], w_expert_up[0], w_expert_down[0])
    y_prompt = moe_combine(yb, dest, x1, route, gt2, g_final)

    xs2 = x_sample.reshape(ns, d)
    to_feature_major = lambda c: jnp.transpose(c, (0, 2, 3, 1)).reshape(c.shape[0], D_ATT, WIN_MAX)
    from_feature_major = lambda c: jnp.transpose(c.reshape(-1, N_HEADS, HEAD_DIM, WIN_MAX), (0, 3, 1, 2))[None]
    sq, sk, sv, sglu = sample_inproj(xs2, ssc1, ssh1, g_norm1[0], w_in_b)
    ko, vo, att_s = sample_cache_attention(sq, sk, sv, to_feature_major(cache_k[0]), to_feature_major(cache_v[0]),
                                           *sample_bias_tables(rel_bias))
    st_taps = jnp.transpose(state_conv[0], (1, 0, 2))
    y_s = sample_tail(att_s, sglu, st_taps, xs2, sgt1, ssc2, ssh2, sgt2, conv_w[0], conv_b[0], ln_g[0], ln_b[0],
                      w_out_b, g_norm2[0], wr_hi, wr_lo, br, g_final, w_expert_gate[0], w_expert_up[0],
                      w_expert_down[0])
    new_conv_s = jnp.transpose(jnp.concatenate([st_taps[1:], sglu[None]], axis=0), (1, 0, 2))

    return (y_prompt, y_s.reshape(ns, 1, d),
            from_feature_major(kc), from_feature_major(vc), tail[:, CONV_PAD - (CONV_W - 1):][None],
            from_feature_major(ko), from_feature_major(vo), new_conv_s[None])
```

```python
import functools
import math

import numpy as np
import jax
import jax.numpy as jnp
from jax import lax
from jax.experimental import pallas as pl
from jax.experimental.pallas import tpu as pltpu

F32 = jnp.float32
BF16 = jnp.bfloat16

D_MODEL = 1024
HEAD_DIM = 64
N_HEADS = 8
D_ATT = N_HEADS * HEAD_DIM
D_CONV = D_MODEL - D_ATT
DILATED_CFGS = ((128, 1), (512, 4), (2048, 16))
WIN_MAX = max(w for w, _ in DILATED_CFGS)
ATT_BLK = 128
CONV_W = 31
NUM_BUCKETS = 32
MAX_DISTANCE = WIN_MAX
N_GROUPS = 4
EXPERTS_PER_GROUP = 8
N_EXPERTS = N_GROUPS * EXPERTS_PER_GROUP
D_EXPERT = 512
EPS = 1e-6
NEG_INF = -1e30
PAST_LEN = 8192

LANES = 128
ROW_TILE = 8
CONV_PAD = 32
CONV_TAIL = 16
ROUTE_LANES = 128
GROUP_LANE0 = N_EXPERTS
MOE_BLK = 512
VMEM_LIMIT = 56 * 1024 * 1024


def _cparams(sem):
    return pltpu.CompilerParams(dimension_semantics=sem, vmem_limit_bytes=VMEM_LIMIT)


def _sigmoid(x):
    return 1.0 / (1.0 + jnp.exp(-x))


def _silu(x):
    return x * _sigmoid(x)


def _rms_scale(x):
    return x * lax.rsqrt(jnp.mean(x * x, axis=-1, keepdims=True) + EPS)


def _dot(a, b):
    return jnp.dot(a, b, preferred_element_type=F32)


def _ada_body(c_ref, w_ref, b_ref, o_ref):
    s = _silu(c_ref[...]).astype(BF16)
    o_ref[...] = _dot(s, w_ref[...].astype(BF16)) + b_ref[...]


def ada_modulation(c, w_ada, b_ada):
    rows, d = c.shape
    n = w_ada.shape[1]
    tn = 1024
    return pl.pallas_call(
        _ada_body,
        grid=(n // tn,),
        in_specs=[pl.BlockSpec((rows, d), lambda j: (0, 0)),
                  pl.BlockSpec((d, tn), lambda j: (0, j)),
                  pl.BlockSpec((1, tn), lambda j: (0, j))],
        out_specs=pl.BlockSpec((rows, tn), lambda j: (0, j)),
        out_shape=jax.ShapeDtypeStruct((rows, n), F32),
        compiler_params=_cparams(("arbitrary",)),
        name="ada_modulation",
    )(c, w_ada, b_ada.reshape(1, n))


def _conv_ln_swish(cbuf, tm, cw_ref, cb_ref, lg_ref, lb_ref, out_ref):
    chunk = 64
    base0 = CONV_PAD - (CONV_W - 1)
    for c in range(tm // chunk):
        pieces = []
        for lt in range(D_CONV // LANES):
            ls = slice(lt * LANES, (lt + 1) * LANES)
            acc = jnp.broadcast_to(cb_ref[:, ls], (chunk, LANES))
            for phase in range(ROW_TILE):
                part = None
                for j in range(CONV_W):
                    if (base0 + j) % ROW_TILE != phase:
                        continue
                    r0 = c * chunk + base0 + j - phase
                    term = cw_ref[j:j + 1, ls] * cbuf[r0:r0 + chunk + ROW_TILE, ls]
                    part = term if part is None else part + term
                if part is not None:
                    acc = acc + part[phase:phase + chunk, :]
            pieces.append(acc)
        acc = jnp.concatenate(pieces, axis=-1)
        mu = jnp.mean(acc, axis=-1, keepdims=True)
        xc = acc - mu
        var = jnp.mean(xc * xc, axis=-1, keepdims=True)
        y = xc * lax.rsqrt(var + EPS) * lg_ref[...] + lb_ref[...]
        out_ref[0, c * chunk:(c + 1) * chunk, :] = _silu(y).astype(out_ref.dtype)


def _lane_slabs_store(slab, rows):
    for c in range(rows.shape[1] // LANES):
        slab[c] = rows[:, c * LANES:(c + 1) * LANES]


def _residue_rows(slab, r, dilation):
    n = slab.shape[1] // dilation
    return jnp.concatenate([slab[c, pl.ds(r, n, stride=dilation), :] for c in range(slab.shape[0])], axis=-1)


def _store_residue_major(slab, rows, out_refs):
    _lane_slabs_store(slab, rows)
    for (_, dilation), ref in zip(DILATED_CFGS, out_refs):
        if dilation == 1:
            ref[0] = rows.astype(ref.dtype)
        else:
            for r in range(dilation):
                ref[0, r] = _residue_rows(slab, r, dilation).astype(ref.dtype)


def _inproj_body(x_ref, sc_ref, sh_ref, g_ref, w_ref, cw_ref, cb_ref, lg_ref, lb_ref,
                 q1, q4, q16, k1, k4, k16, v1, v4, v16, kc_ref, vc_ref, conv_ref, tail_ref, cbuf, slab):
    i = pl.program_id(1)
    tm = x_ref.shape[1]

    @pl.when(i == 0)
    def _():
        cbuf[0:CONV_PAD, :] = jnp.zeros((CONV_PAD, D_CONV), F32)
        cbuf[CONV_PAD + tm:, :] = jnp.zeros((CONV_TAIL, D_CONV), F32)

    x = x_ref[0]
    h = (_rms_scale(x) * g_ref[...] * (1.0 + sc_ref[0]) + sh_ref[0]).astype(BF16)

    def col(j):
        return _dot(h, w_ref[:, j * D_ATT:(j + 1) * D_ATT])

    _store_residue_major(slab, col(0) * (HEAD_DIM ** -0.5), (q1, q4, q16))
    k = col(1)
    _store_residue_major(slab, k, (k1, k4, k16))
    v = col(2)
    _store_residue_major(slab, v, (v1, v4, v16))

    kc_ref[0] = k.T
    vc_ref[0] = v.T

    glu = col(3) * _sigmoid(col(4))
    cbuf[CONV_PAD:CONV_PAD + tm, :] = glu
    tail_ref[0] = glu[tm - CONV_PAD:, :]
    _conv_ln_swish(cbuf, tm, cw_ref, cb_ref, lg_ref, lb_ref, conv_ref)
    cbuf[0:CONV_PAD, :] = cbuf[tm:tm + CONV_PAD, :]


def prompt_inproj(x, sc1, sh1, g1, w_in_bf16, conv_w, conv_b, ln_g, ln_b, tm=512):
    b, s, d = x.shape
    nt = s // tm
    first_cache_tile = (s - WIN_MAX) // tm
    row = lambda v: v.reshape(1, -1)
    tok = lambda bi, i: (bi, i, 0)
    per_b = lambda bi, i: (bi, 0, 0)
    cst = lambda bi, i: (0, 0)
    cache = lambda bi, i: (bi, 0, jnp.maximum(i - first_cache_tile, 0))
    qkv_specs, qkv_shapes = [], []
    for _ in range(3):
        for _, dil in DILATED_CFGS:
            if dil == 1:
                qkv_specs.append(pl.BlockSpec((1, tm, D_ATT), tok))
                qkv_shapes.append(jax.ShapeDtypeStruct((b, s, D_ATT), BF16))
            else:
                qkv_specs.append(pl.BlockSpec((1, dil, tm // dil, D_ATT), lambda bi, i: (bi, 0, i, 0)))
                qkv_shapes.append(jax.ShapeDtypeStruct((b, dil, s // dil, D_ATT), BF16))
    outs = pl.pallas_call(
        _inproj_body,
        grid=(b, nt),
        in_specs=[pl.BlockSpec((1, tm, d), tok),
                  pl.BlockSpec((1, 1, d), per_b),
                  pl.BlockSpec((1, 1, d), per_b),
                  pl.BlockSpec((1, d), cst),
                  pl.BlockSpec(w_in_bf16.shape, cst),
                  pl.BlockSpec(conv_w.shape, cst),
                  pl.BlockSpec((1, D_CONV), cst),
                  pl.BlockSpec((1, D_CONV), cst),
                  pl.BlockSpec((1, D_CONV), cst)],
        out_specs=qkv_specs + [pl.BlockSpec((1, D_ATT, tm), cache),
                               pl.BlockSpec((1, D_ATT, tm), cache),
                               pl.BlockSpec((1, tm, D_CONV), tok),
                               pl.BlockSpec((1, CONV_PAD, D_CONV), per_b)],
        out_shape=qkv_shapes + [jax.ShapeDtypeStruct((b, D_ATT, WIN_MAX), F32),
                                jax.ShapeDtypeStruct((b, D_ATT, WIN_MAX), F32),
                                jax.ShapeDtypeStruct((b, s, D_CONV), BF16),
                                jax.ShapeDtypeStruct((b, CONV_PAD, D_CONV), F32)],
        scratch_shapes=[pltpu.VMEM((tm + CONV_PAD + CONV_TAIL, D_CONV), F32),
                        pltpu.VMEM((D_ATT // LANES, tm, LANES), F32)],
        compiler_params=_cparams(("arbitrary", "arbitrary")),
        name="prompt_inproj_conv",
    )(x, sc1, sh1, row(g1), w_in_bf16, conv_w, row(conv_b), row(ln_g), row(ln_b))
    nd = len(DILATED_CFGS)
    return outs[0:nd], outs[nd:2 * nd], outs[2 * nd:3 * nd], outs[3 * nd:]


def _attn_body(q_ref, kc_ref, kp_ref, vc_ref, vp_ref, bias_ref, hmask_ref, o_ref, lse_ref, *, wsub):
    i = pl.program_id(2)
    nq = q_ref.shape[0] // ATT_BLK
    npair = N_HEADS // 2
    qi = lax.broadcasted_iota(jnp.int32, (ATT_BLK, 2 * ATT_BLK), 0)
    kj = lax.broadcasted_iota(jnp.int32, (ATT_BLK, 2 * ATT_BLK), 1)
    rel = qi + ATT_BLK - kj
    band = (rel >= 0) & (rel <= wsub)
    band_first = band & ((kj >= ATT_BLK) | (i > 0))
    low = lax.broadcasted_iota(jnp.int32, (ATT_BLK, LANES), 1) < HEAD_DIM
    for sub in range(nq):
        r0 = sub * ATT_BLK
        valid = band_first if sub == 0 else band
        scores, values = [], []
        for hp in range(npair):
            cs = slice(hp * LANES, (hp + 1) * LANES)
            qp = q_ref[r0:r0 + ATT_BLK, cs]
            if sub == 0:
                kk = jnp.concatenate([kp_ref[:, cs], kc_ref[0:ATT_BLK, cs]], axis=0)
                vv = jnp.concatenate([vp_ref[:, cs], vc_ref[0:ATT_BLK, cs]], axis=0)
            else:
                kk = kc_ref[r0 - ATT_BLK:r0 + ATT_BLK, cs]
                vv = vc_ref[r0 - ATT_BLK:r0 + ATT_BLK, cs]
            values.append(vv)
            for half in range(2):
                s = lax.dot_general(qp * hmask_ref[half], kk, (((1,), (1,)), ((), ())), preferred_element_type=F32)
                scores.append(jnp.where(valid, s + bias_ref[2 * hp + half], NEG_INF))
        s = jnp.concatenate(scores, axis=0)
        m = jnp.max(s, axis=-1, keepdims=True)
        p = jnp.exp(s - m)
        l = jnp.sum(p, axis=-1, keepdims=True)
        p16 = p.astype(BF16)
        inv = 1.0 / l
        lse = m + jnp.log(l)
        for hp in range(npair):
            cs = slice(hp * LANES, (hp + 1) * LANES)
            ra = slice(2 * hp * ATT_BLK, (2 * hp + 1) * ATT_BLK)
            rb = slice((2 * hp + 1) * ATT_BLK, (2 * hp + 2) * ATT_BLK)
            oa = _dot(p16[ra], values[hp]) * inv[ra]
            ob = _dot(p16[rb], values[hp]) * inv[rb]
            o_ref[r0:r0 + ATT_BLK, cs] = jnp.where(low, oa, ob).astype(o_ref.dtype)
            lse_ref[r0:r0 + ATT_BLK, cs] = jnp.where(low, lse[ra], lse[rb])


def attention_branch(q, k, v, bias, window, dilation, qblocks=4):
    c = q.shape[-1]
    b, n = q.shape[0], q.shape[-2]
    wsub = window // dilation
    tq = ATT_BLK * qblocks
    if dilation == 1:
        cur_spec = pl.BlockSpec((None, tq, c), lambda bi, r, i: (bi, i, 0))
        prev_spec = pl.BlockSpec((None, ATT_BLK, c), lambda bi, r, i: (bi, jnp.maximum(i * qblocks - 1, 0), 0))
    else:
        cur_spec = pl.BlockSpec((None, None, tq, c), lambda bi, r, i: (bi, r, i, 0))
        prev_spec = pl.BlockSpec((None, None, ATT_BLK, c),
                                 lambda bi, r, i: (bi, r, jnp.maximum(i * qblocks - 1, 0), 0))
    lane = jnp.arange(LANES)[None, :]
    hmask = jnp.broadcast_to(jnp.stack([lane < HEAD_DIM, lane >= HEAD_DIM]).astype(BF16), (2, ATT_BLK, LANES))
    return pl.pallas_call(
        functools.partial(_attn_body, wsub=wsub),
        grid=(b, dilation, n // tq),
        in_specs=[cur_spec, cur_spec, prev_spec, cur_spec, prev_spec,
                  pl.BlockSpec(bias.shape, lambda bi, r, i: (0, 0, 0)),
                  pl.BlockSpec(hmask.shape, lambda bi, r, i: (0, 0, 0))],
        out_specs=[cur_spec, cur_spec],
        out_shape=[jax.ShapeDtypeStruct(q.shape, BF16), jax.ShapeDtypeStruct(q.shape, F32)],
        compiler_params=_cparams(("arbitrary", "arbitrary", "arbitrary")),
        name=f"attn_branch_d{dilation}",
    )(q, k, k, v, v, bias, hmask)


def _split_bf16(a):
    hi = a.astype(BF16)
    lo = (a - hi.astype(F32)).astype(BF16)
    return hi, lo


def _router_logits(h2, wr_hi_ref, wr_lo_ref, br_ref):
    hi, lo = _split_bf16(h2)
    return _dot(hi, wr_hi_ref[...]) + (_dot(hi, wr_lo_ref[...]) + _dot(lo, wr_hi_ref[...])) + br_ref[...]


def _route(logits):
    rows = logits.shape[0]
    lane = lax.broadcasted_iota(jnp.int32, (rows, ROUTE_LANES), 1)
    lanef = lane.astype(F32)
    big = float(ROUTE_LANES)
    is_g = (lane >= GROUP_LANE0) & (lane < GROUP_LANE0 + N_GROUPS)
    glog = jnp.where(is_g, logits, -jnp.inf)
    gmax = jnp.max(glog, axis=-1, keepdims=True)
    gsum = jnp.sum(jnp.exp(glog - gmax), axis=-1, keepdims=True)
    p_grp = 1.0 / gsum
    glane = jnp.min(jnp.where(glog == gmax, lanef, big), axis=-1, keepdims=True)
    grp = glane - float(GROUP_LANE0)
    in_grp = jnp.floor(lanef * (1.0 / EXPERTS_PER_GROUP)) == grp
    elog = jnp.where(in_grp, logits, -jnp.inf)
    emax = jnp.max(elog, axis=-1, keepdims=True)
    eexp = jnp.exp(elog - emax)
    pe = eexp / jnp.sum(eexp, axis=-1, keepdims=True)
    pe = jnp.where(in_grp, pe, -1.0)
    p1 = jnp.max(pe, axis=-1, keepdims=True)
    e1 = jnp.min(jnp.where(pe == p1, lanef, big), axis=-1, keepdims=True)
    pe2 = jnp.where(lanef == e1, -1.0, pe)
    p2 = jnp.max(pe2, axis=-1, keepdims=True)
    e2 = jnp.min(jnp.where(pe2 == p2, lanef, big), axis=-1, keepdims=True)
    psum = p1 + p2
    return e1, e2, p_grp * p1 / psum, p_grp * p2 / psum


def _position_order(ref, slab, dilation):
    if dilation == 1:
        return ref[0].astype(F32)
    n = ref.shape[2]
    for r in range(dilation):
        piece = ref[0, r].astype(F32)
        for c in range(slab.shape[0]):
            slab[c, pl.ds(r, n, stride=dilation), :] = piece[:, c * LANES:(c + 1) * LANES]
    return jnp.concatenate([slab[c] for c in range(slab.shape[0])], axis=-1)


def _mix_branches(o_refs, lse_refs, slabs):
    slabs = iter(slabs)
    lses = [_position_order(r, None if d == 1 else next(slabs), d) for r, (_, d) in zip(lse_refs, DILATED_CFGS)]
    outs = [_position_order(r, None if d == 1 else next(slabs), d) for r, (_, d) in zip(o_refs, DILATED_CFGS)]
    mx = jnp.maximum(jnp.maximum(lses[0], lses[1]), lses[2])
    ws = [jnp.exp(l - mx) for l in lses]
    numer = sum(w * o for w, o in zip(ws, outs))
    return numer / (ws[0] + ws[1] + ws[2])


def _outproj_body(o1, o4, o16, l1, l4, l16, conv_ref, x_ref, gt_ref, sc_ref, sh_ref, g_ref,
                  wo_ref, wrh_ref, wrl_ref, br_ref,
                  x1_ref, h2_ref, route_ref, cnt_ref, carry, *slabs):
    first = (pl.program_id(0) == 0) & (pl.program_id(1) == 0)

    @pl.when(first)
    def _():
        carry[...] = jnp.zeros_like(carry)

    tm = x_ref.shape[1]
    att = _mix_branches((o1, o4, o16), (l1, l4, l16), slabs).astype(BF16)
    mix = _dot(att, wo_ref[0:D_ATT, :]) + _dot(conv_ref[0], wo_ref[D_ATT:, :])
    x1 = x_ref[0] + gt_ref[0] * mix
    x1_ref[0] = x1
    h2 = _rms_scale(x1) * g_ref[...] * (1.0 + sc_ref[0]) + sh_ref[0]
    h2_ref[0] = h2.astype(BF16)
    e1, e2, w1, w2 = _route(_router_logits(h2, wrh_ref, wrl_ref, br_ref))

    lane = lax.broadcasted_iota(jnp.int32, (tm, ROUTE_LANES), 1).astype(F32)
    oh1 = lane == e1
    oh2 = lane == e2
    onehot = jnp.where(oh1 | oh2, 1.0, 0.0)
    ri = lax.broadcasted_iota(jnp.int32, (tm, tm), 0)
    ci = lax.broadcasted_iota(jnp.int32, (tm, tm), 1)
    tri = jnp.where(ci < ri, 1.0, 0.0).astype(BF16)
    before = _dot(tri, onehot.astype(BF16)) + carry[...]
    r1 = jnp.sum(jnp.where(oh1, before, 0.0), axis=-1, keepdims=True)
    r2 = jnp.sum(jnp.where(oh2, before, 0.0), axis=-1, keepdims=True)
    carry[...] = carry[...] + jnp.sum(onehot, axis=0, keepdims=True)
    cnt_ref[...] = carry[...]
    rec = jnp.where(lane == 0.0, e1, 0.0)
    for idx, val in ((1, e2), (2, w1), (3, w2), (4, r1), (5, r2)):
        rec = jnp.where(lane == float(idx), val, rec)
    route_ref[0] = rec


def prompt_outproj(o_list, lse_list, conv_act, x, gt1, sc2, sh2, g2, w_out_bf16, wr_hi, wr_lo, br, tm=512):
    b, s, d = x.shape
    tok = lambda bi, i: (bi, i, 0)
    per_b = lambda bi, i: (bi, 0, 0)
    cst = lambda bi, i: (0, 0)
    half = pl.BlockSpec((1, tm, D_ATT), tok)
    branch_specs = [half if dil == 1 else pl.BlockSpec((1, dil, tm // dil, D_ATT), lambda bi, i: (bi, 0, i, 0))
                    for _, dil in DILATED_CFGS]
    n_slabs = 2 * sum(1 for _, dil in DILATED_CFGS if dil > 1)
    return pl.pallas_call(
        _outproj_body,
        grid=(b, s // tm),
        in_specs=branch_specs * 2 + [half, pl.BlockSpec((1, tm, d), tok),
                               pl.BlockSpec((1, 1, d), per_b),
                               pl.BlockSpec((1, 1, d), per_b),
                               pl.BlockSpec((1, 1, d), per_b),
                               pl.BlockSpec((1, d), cst),
                               pl.BlockSpec(w_out_bf16.shape, cst),
                               pl.BlockSpec(wr_hi.shape, cst),
                               pl.BlockSpec(wr_lo.shape, cst),
                               pl.BlockSpec((1, ROUTE_LANES), cst)],
        out_specs=[pl.BlockSpec((1, tm, d), tok),
                   pl.BlockSpec((1, tm, d), tok),
                   pl.BlockSpec((1, tm, ROUTE_LANES), tok),
                   pl.BlockSpec((1, ROUTE_LANES), cst)],
        out_shape=[jax.ShapeDtypeStruct((b, s, d), F32),
                   jax.ShapeDtypeStruct((b, s, d), BF16),
                   jax.ShapeDtypeStruct((b, s, ROUTE_LANES), F32),
                   jax.ShapeDtypeStruct((1, ROUTE_LANES), F32)],
        scratch_shapes=[pltpu.VMEM((1, ROUTE_LANES), F32)] + [pltpu.VMEM((D_ATT // LANES, tm, LANES), F32)] * n_slabs,
        compiler_params=_cparams(("arbitrary", "arbitrary")),
        name="prompt_outproj_router",
    )(*o_list, *lse_list, conv_act, x, gt1, sc2, sh2, g2.reshape(1, d), w_out_bf16, wr_hi, wr_lo, br)


def router_weights(w_rg, b_rg, w_re, b_re):
    d = w_rg.shape[0]
    fill = ROUTE_LANES - N_EXPERTS - N_GROUPS
    w = jnp.concatenate([w_re.astype(F32), w_rg.astype(F32), jnp.zeros((d, fill), F32)], axis=1)
    bias = jnp.concatenate([b_re.astype(F32), b_rg.astype(F32), jnp.zeros((fill,), F32)])[None, :]
    hi = w.astype(BF16)
    lo = (w - hi.astype(F32)).astype(BF16)
    return hi, lo, bias


def _rows_from_tiles(tile_ref, n):
    return jnp.concatenate([tile_ref[pl.ds(c, n, stride=ROW_TILE), :] for c in range(ROW_TILE)], axis=-1)


def _rows_to_tiles(tile_ref, rows):
    n = rows.shape[0]
    for c in range(ROW_TILE):
        tile_ref[pl.ds(c, n, stride=ROW_TILE), :] = rows[:, c * LANES:(c + 1) * LANES]


def _dispatch_body(dest_ref, prev_ref, pend_ref, h2_ref, xs_hbm, tiles, zbuf, sems, zsem, *, chunk, nsteps):
    step = pl.program_id(0)

    @pl.when(step == 0)
    def _():
        zbuf[...] = jnp.zeros_like(zbuf)

        def zcopy(e):
            start = pl.multiple_of(jnp.maximum(pend_ref[e + 1] - MOE_BLK * ROW_TILE, 0), ROW_TILE)
            return pltpu.make_async_copy(zbuf, xs_hbm.at[pl.ds(start, MOE_BLK * ROW_TILE)], zsem)

        def zstart(e, c):
            @pl.when(pend_ref[e + 1] > pend_ref[e])
            def _():
                zcopy(e).start()
            return c

        def zwait(e, c):
            @pl.when(pend_ref[e + 1] > pend_ref[e])
            def _():
                zcopy(e).wait()
            return c

        lax.fori_loop(0, N_EXPERTS, zstart, 0)
        lax.fori_loop(0, N_EXPERTS, zwait, 0)

        blk_rows = MOE_BLK * ROW_TILE
        nblk = xs_hbm.shape[0] // blk_rows

        def tcopy(i):
            return pltpu.make_async_copy(zbuf, xs_hbm.at[pl.ds(pl.multiple_of(i * blk_rows, blk_rows), blk_rows)],
                                         zsem)

        def tstart(i, c):
            @pl.when(i * blk_rows >= pend_ref[N_EXPERTS])
            def _():
                tcopy(i).start()
            return c

        def twait(i, c):
            @pl.when(i * blk_rows >= pend_ref[N_EXPERTS])
            def _():
                tcopy(i).wait()
            return c

        lax.fori_loop(0, nblk, tstart, 0)
        lax.fori_loop(0, nblk, twait, 0)

    def copy(dref, slot, t, k):
        src = pl.multiple_of(t * ROW_TILE, ROW_TILE)
        dst = pl.multiple_of(dref[0, 0, 2 * t + k], ROW_TILE)
        return pltpu.make_async_copy(tiles.at[slot, pl.ds(src, ROW_TILE)], xs_hbm.at[pl.ds(dst, ROW_TILE)],
                                     sems.at[slot])

    def issue(dref, slot):
        def body(t, c):
            copy(dref, slot, t, 0).start(priority=0)
            copy(dref, slot, t, 1).start(priority=1)
            return c
        lax.fori_loop(0, chunk, body, 0, unroll=8)

    def drain(dref, slot):
        def body(t, c):
            copy(dref, slot, t, 0).wait()
            copy(dref, slot, t, 1).wait()
            return c
        lax.fori_loop(0, chunk, body, 0, unroll=8)

    for slot in range(2):
        @pl.when(lax.rem(step, 2) == slot)
        def _(slot=slot):
            _rows_to_tiles(tiles.at[slot], h2_ref[...].astype(F32))
            issue(dest_ref, slot)

            @pl.when(step > 0)
            def _():
                drain(prev_ref, 1 - slot)

            @pl.when(step == nsteps - 1)
            def _():
                drain(dest_ref, slot)


def moe_dispatch(h2_flat, dest_rt, pends_rt, nslots, chunk=512):
    t, d = h2_flat.shape
    assert d == ROW_TILE * LANES
    nsteps = t // chunk
    dest3 = dest_rt.reshape(nsteps, 1, 2 * chunk)
    return pl.pallas_call(
        functools.partial(_dispatch_body, chunk=chunk, nsteps=nsteps),
        grid=(nsteps,),
        in_specs=[pl.BlockSpec((1, 1, 2 * chunk), lambda i: (i, 0, 0), memory_space=pltpu.SMEM),
                  pl.BlockSpec((1, 1, 2 * chunk), lambda i: (jnp.maximum(i - 1, 0), 0, 0), memory_space=pltpu.SMEM),
                  pl.BlockSpec(memory_space=pltpu.SMEM),
                  pl.BlockSpec((chunk, d), lambda i: (i, 0))],
        out_specs=pl.BlockSpec(memory_space=pl.ANY),
        out_shape=jax.ShapeDtypeStruct((nslots * ROW_TILE, LANES), F32),
        scratch_shapes=[pltpu.VMEM((2, chunk * ROW_TILE, LANES), F32),
                        pltpu.VMEM((MOE_BLK * ROW_TILE, LANES), F32),
                        pltpu.SemaphoreType.DMA((2,)),
                        pltpu.SemaphoreType.DMA(())],
        compiler_params=_cparams(("arbitrary",)),
        name="moe_dispatch",
    )(dest3, dest3, pends_rt, h2_flat)


def _experts_body(blk_e_ref, nused_ref, xs_ref, w1_ref, w3_ref, w2_ref, y_ref, w1b, w3b, w2b):
    i = pl.program_id(0)
    new_expert = (i == 0) | (blk_e_ref[i] != blk_e_ref[jnp.maximum(i - 1, 0)])

    @pl.when((i < nused_ref[0]) & new_expert)
    def _():
        w1b[...] = w1_ref[0].astype(BF16)
        w3b[...] = w3_ref[0].astype(BF16)
        w2b[...] = w2_ref[0].astype(BF16)

    @pl.when(i < nused_ref[0])
    def _():
        xb = _rows_from_tiles(xs_ref, MOE_BLK).astype(BF16)
        a = (_silu(_dot(xb, w1b[...])) * _dot(xb, w3b[...])).astype(BF16)
        _rows_to_tiles(y_ref, _dot(a, w2b[...]))

    @pl.when(i >= nused_ref[0])
    def _():
        y_ref[...] = jnp.zeros_like(y_ref)


def moe_experts(xs, blk_e, nused, w_e1, w_e3, w_e2):
    nb = xs.shape[0] // (MOE_BLK * ROW_TILE)
    _, d, de = w_e1.shape
    blk = lambda i, be, nu: (jnp.minimum(i, nu[0] - 1), 0)
    wsel = lambda i, be, nu: (be[i], 0, 0)
    grid_spec = pltpu.PrefetchScalarGridSpec(
        num_scalar_prefetch=2,
        grid=(nb,),
        in_specs=[pl.BlockSpec((MOE_BLK * ROW_TILE, LANES), blk),
                  pl.BlockSpec((1, d, de), wsel),
                  pl.BlockSpec((1, d, de), wsel),
                  pl.BlockSpec((1, de, d), wsel)],
        out_specs=pl.BlockSpec((MOE_BLK * ROW_TILE, LANES), lambda i, be, nu: (i, 0)),
        scratch_shapes=[pltpu.VMEM((d, de), BF16), pltpu.VMEM((d, de), BF16), pltpu.VMEM((de, d), BF16)],
    )
    return pl.pallas_call(
        _experts_body,
        grid_spec=grid_spec,
        out_shape=jax.ShapeDtypeStruct(xs.shape, F32),
        compiler_params=_cparams(("arbitrary",)),
        name="moe_experts",
    )(blk_e, nused, xs, w_e1, w_e3, w_e2)


def _combine_body(dest_ref, next_ref, yb_hbm, x1_ref, route_ref, gt_ref, gf_ref, out_ref, ya0, ya1, sems,
                  *, tm, nsteps):
    step = pl.program_id(0) * pl.num_programs(1) + pl.program_id(1)

    def copy(dref, slot, t, k):
        src = pl.multiple_of(dref[0, 0, 2 * t + k], ROW_TILE)
        dst = pl.multiple_of(t * ROW_TILE, ROW_TILE)
        buf = ya0 if k == 0 else ya1
        return pltpu.make_async_copy(yb_hbm.at[pl.ds(src, ROW_TILE)], buf.at[slot, pl.ds(dst, ROW_TILE)],
                                     sems.at[slot])

    def issue(dref, slot):
        def body(t, c):
            copy(dref, slot, t, 0).start(priority=0)
            copy(dref, slot, t, 1).start(priority=1)
            return c
        lax.fori_loop(0, tm, body, 0, unroll=8)

    def drain(dref, slot):
        def body(t, c):
            copy(dref, slot, t, 0).wait()
            copy(dref, slot, t, 1).wait()
            return c
        lax.fori_loop(0, tm, body, 0, unroll=8)

    for slot in range(2):
        @pl.when(lax.rem(step, 2) == slot)
        def _(slot=slot):
            @pl.when(step == 0)
            def _():
                issue(dest_ref, slot)

            @pl.when(step + 1 < nsteps)
            def _():
                issue(next_ref, 1 - slot)

            drain(dest_ref, slot)
            rec = route_ref[0]
            moe = (rec[:, 2:3] * _rows_from_tiles(ya0.at[slot], tm)
                   + rec[:, 3:4] * _rows_from_tiles(ya1.at[slot], tm))
            y = x1_ref[0] + gt_ref[0] * moe
            out_ref[0] = _rms_scale(y) * gf_ref[...]


def moe_combine(yb, dest_rt, x1, route, gt2, g_final, tm=256):
    b, s, d = x1.shape
    nt = s // tm
    dest3 = dest_rt.reshape(b * nt, 1, 2 * tm)
    tok = lambda bi, i: (bi, i, 0)
    nsteps = b * nt
    return pl.pallas_call(
        functools.partial(_combine_body, tm=tm, nsteps=nsteps),
        grid=(b, nt),
        in_specs=[pl.BlockSpec((1, 1, 2 * tm), lambda bi, i: (bi * nt + i, 0, 0), memory_space=pltpu.SMEM),
                  pl.BlockSpec((1, 1, 2 * tm), lambda bi, i: (jnp.minimum(bi * nt + i + 1, nsteps - 1), 0, 0),
                               memory_space=pltpu.SMEM),
                  pl.BlockSpec(memory_space=pl.ANY),
                  pl.BlockSpec((1, tm, d), tok),
                  pl.BlockSpec((1, tm, ROUTE_LANES), tok),
                  pl.BlockSpec((1, 1, d), lambda bi, i: (bi, 0, 0)),
                  pl.BlockSpec((1, d), lambda bi, i: (0, 0))],
        out_specs=pl.BlockSpec((1, tm, d), tok),
        out_shape=jax.ShapeDtypeStruct((b, s, d), F32),
        scratch_shapes=[pltpu.VMEM((2, tm * ROW_TILE, LANES), F32), pltpu.VMEM((2, tm * ROW_TILE, LANES), F32),
                        pltpu.SemaphoreType.DMA((2,))],
        compiler_params=_cparams(("arbitrary", "arbitrary")),
        name="moe_combine_final",
    )(dest3, dest3, yb, x1, route, gt2, g_final.reshape(1, d))


def slot_plan(route, counts, nblocks):
    cnt = counts[0, :N_EXPERTS].astype(jnp.int32)
    padded = (cnt + MOE_BLK - 1) // MOE_BLK * MOE_BLK
    pends = jnp.cumsum(padded)
    pstarts = pends - padded
    eid = route[..., 0:2].astype(jnp.int32)
    rank = route[..., 4:6].astype(jnp.int32)
    experts = jnp.arange(N_EXPERTS, dtype=jnp.int32)
    start_of = jnp.sum(jnp.where(eid[..., None] == experts, pstarts, 0), axis=-1)
    dest = (start_of + rank).reshape(-1)
    blk_start = jnp.arange(nblocks, dtype=jnp.int32) * MOE_BLK
    blk_e = jnp.minimum(jnp.sum((pends[None, :] <= blk_start[:, None]).astype(jnp.int32), axis=1), N_EXPERTS - 1)
    nused = (pends[-1:] // MOE_BLK).astype(jnp.int32)
    pend_ext = jnp.concatenate([jnp.zeros((1,), jnp.int32), pends.astype(jnp.int32)])
    return (dest * ROW_TILE).astype(jnp.int32), blk_e, nused, (pend_ext * ROW_TILE).astype(jnp.int32)


def _sample_inproj_body(x_ref, sc_ref, sh_ref, g_ref, w_ref, q_ref, k_ref, v_ref, glu_ref):
    h = (_rms_scale(x_ref[...]) * g_ref[...] * (1.0 + sc_ref[...]) + sh_ref[...]).astype(BF16)

    def col(j):
        return _dot(h, w_ref[:, j * D_ATT:(j + 1) * D_ATT])

    q_ref[...] = col(0) * (HEAD_DIM ** -0.5)
    k_ref[...] = col(1)
    v_ref[...] = col(2)
    glu_ref[...] = col(3) * _sigmoid(col(4))


def sample_inproj(x, sc1, sh1, g1, w_in_bf16):
    n, d = x.shape
    out = jax.ShapeDtypeStruct((n, D_ATT), F32)
    return pl.pallas_call(
        _sample_inproj_body,
        out_shape=[out, out, out, out],
        compiler_params=pltpu.CompilerParams(vmem_limit_bytes=VMEM_LIMIT),
        name="sample_inproj",
    )(x, sc1, sh1, g1.reshape(1, d), w_in_bf16)


def _sample_cache_body(qt_ref, knt_ref, vnt_ref, k_ref, v_ref, bias_ref, valid_ref, bias0_ref, valid0_ref,
                       ko_ref, vo_ref, att_ref, s_s, p_s, n_s):
    b = pl.program_id(0)
    win = k_ref.shape[2]
    sel = lax.broadcasted_iota(jnp.int32, qt_ref.shape, 1) == b

    def column(ref):
        return jnp.sum(jnp.where(sel, ref[...], 0.0), axis=1, keepdims=True)

    q, kn, vn = column(qt_ref), column(knt_ref), column(vnt_ref)
    kt = k_ref[0]
    vt = v_ref[0]
    for h in range(N_HEADS):
        rows = slice(h * HEAD_DIM, (h + 1) * HEAD_DIM)
        s_s[h:h + 1, :] = jnp.sum(kt[rows, :] * q[rows, :], axis=0, keepdims=True)
        n_s[h:h + 1, :] = jnp.broadcast_to(jnp.sum(kn[rows, :] * q[rows, :], axis=0, keepdims=True), (1, LANES))
    s_all = s_s[...]
    s_new = n_s[...][:, 0:1] + bias0_ref[:, 0:1]
    parts = []
    for br in range(len(DILATED_CFGS)):
        s = jnp.where(valid_ref[br] > 0.0, s_all + bias_ref[br], NEG_INF)
        s0 = jnp.where(valid0_ref[br][:, 0:1] > 0.0, s_new, NEG_INF)
        m = jnp.maximum(jnp.max(s, axis=-1, keepdims=True), s0)
        p = jnp.exp(s - m)
        p0 = jnp.exp(s0 - m)
        parts.append((p, p0, m, jnp.sum(p, axis=-1, keepdims=True) + p0))
    mx = jnp.maximum(jnp.maximum(parts[0][2], parts[1][2]), parts[2][2])
    ws = [jnp.exp(m - mx) for _, _, m, _ in parts]
    inv = 1.0 / sum(w * l for w, (_, _, _, l) in zip(ws, parts))
    p_s[...] = sum(w * p for w, (p, _, _, _) in zip(ws, parts)) * inv
    p_new = sum(w * p0 for w, (_, p0, _, _) in zip(ws, parts)) * inv
    cols = []
    for h in range(N_HEADS):
        rows = slice(h * HEAD_DIM, (h + 1) * HEAD_DIM)
        cols.append(jnp.sum(vt[rows, :] * p_s[h:h + 1, :], axis=1, keepdims=True) + p_new[h:h + 1, :] * vn[rows, :])
    att = jnp.concatenate(cols, axis=0)

    @pl.when(b == 0)
    def _():
        att_ref[...] = jnp.zeros_like(att_ref)

    att_ref[...] = jnp.where(sel, att, att_ref[...])
    last = lax.broadcasted_iota(jnp.int32, kt.shape, 1) == win - 1
    ko_ref[0] = jnp.where(last, kn, pltpu.roll(kt, win - 1, 1))
    vo_ref[0] = jnp.where(last, vn, pltpu.roll(vt, win - 1, 1))


def sample_cache_attention(q, k_new, v_new, cache_kt, cache_vt, bias, valid, bias0, valid0):
    n, c, win = cache_kt.shape
    full = lambda a: pl.BlockSpec(a.shape, lambda i: (0,) * a.ndim)
    seq = pl.BlockSpec((1, c, win), lambda i: (i, 0, 0))
    qt, knt, vnt = q.T, k_new.T, v_new.T
    ko, vo, att_t = pl.pallas_call(
        _sample_cache_body,
        grid=(n,),
        in_specs=[full(qt), full(knt), full(vnt), seq, seq, full(bias), full(valid), full(bias0), full(valid0)],
        out_specs=[seq, seq, pl.BlockSpec((c, n), lambda i: (0, 0))],
        out_shape=[jax.ShapeDtypeStruct(cache_kt.shape, F32), jax.ShapeDtypeStruct(cache_vt.shape, F32),
                   jax.ShapeDtypeStruct((c, n), F32)],
        scratch_shapes=[pltpu.VMEM((N_HEADS, win), F32), pltpu.VMEM((N_HEADS, win), F32),
                        pltpu.VMEM((N_HEADS, LANES), F32)],
        compiler_params=_cparams(("arbitrary",)),
        name="sample_cache_attention",
    )(qt, knt, vnt, cache_kt, cache_vt, bias, valid, bias0, valid0)
    return ko, vo, att_t.T


def _sample_tail_body(att_ref, glu_ref, st_ref, x_ref, gt1_ref, sc2_ref, sh2_ref, gt2_ref,
                      cw_ref, cb_ref, lg_ref, lb_ref, wo_ref, g2_ref, wrh_ref, wrl_ref, br_ref, gf_ref,
                      w1_ref, w3_ref, w2_ref, y_ref, x1_s, h2_s, gate_s, acc_s):
    e = pl.program_id(0)
    n = x_ref.shape[0]

    @pl.when(e == 0)
    def _():
        nhist = CONV_W - 1
        conv = cw_ref[nhist:CONV_W, :] * glu_ref[...] + cb_ref[...]
        for j in range(nhist):
            conv = conv + cw_ref[j:j + 1, :] * st_ref[j]
        mu = jnp.mean(conv, axis=-1, keepdims=True)
        xc = conv - mu
        var = jnp.mean(xc * xc, axis=-1, keepdims=True)
        act = _silu(xc * lax.rsqrt(var + EPS) * lg_ref[...] + lb_ref[...]).astype(BF16)
        mix = _dot(att_ref[...].astype(BF16), wo_ref[0:D_ATT, :]) + _dot(act, wo_ref[D_ATT:, :])
        x1 = x_ref[...] + gt1_ref[...] * mix
        x1_s[...] = x1
        h2 = _rms_scale(x1) * g2_ref[...] * (1.0 + sc2_ref[...]) + sh2_ref[...]
        h2_s[...] = h2.astype(BF16)
        e1, e2, w1, w2 = _route(_router_logits(h2, wrh_ref, wrl_ref, br_ref))
        lane = lax.broadcasted_iota(jnp.int32, (n, ROUTE_LANES), 1).astype(F32)
        gate_s[...] = jnp.where(lane == e1, w1, 0.0) + jnp.where(lane == e2, w2, 0.0)
        acc_s[...] = jnp.zeros_like(acc_s)

    hb = h2_s[...]
    a = (_silu(_dot(hb, w1_ref[0].astype(BF16))) * _dot(hb, w3_ref[0].astype(BF16))).astype(BF16)
    ye = _dot(a, w2_ref[0].astype(BF16))
    lane = lax.broadcasted_iota(jnp.int32, (n, ROUTE_LANES), 1)
    g = jnp.sum(jnp.where(lane == e, gate_s[...], 0.0), axis=-1, keepdims=True)
    acc_s[...] = acc_s[...] + g * ye

    @pl.when(e == pl.num_programs(0) - 1)
    def _():
        y = x1_s[...] + gt2_ref[...] * acc_s[...]
        y_ref[...] = _rms_scale(y) * gf_ref[...]


def sample_tail(att, glu, state_conv, x, gt1, sc2, sh2, gt2, conv_w, conv_b, ln_g, ln_b, w_out_bf16, g2,
                wr_hi, wr_lo, br, g_final, w_e1, w_e3, w_e2):
    n, d = x.shape
    de = w_e1.shape[2]
    row = lambda v: v.reshape(1, -1)
    full = lambda a: pl.BlockSpec(a.shape, lambda e: (0,) * a.ndim)
    args = [att, glu, state_conv, x, gt1, sc2, sh2, gt2, conv_w, row(conv_b), row(ln_g), row(ln_b), w_out_bf16,
            row(g2), wr_hi, wr_lo, br, row(g_final)]
    return pl.pallas_call(
        _sample_tail_body,
        grid=(N_EXPERTS,),
        in_specs=[full(a) for a in args] + [pl.BlockSpec((1, d, de), lambda e: (e, 0, 0)),
                                            pl.BlockSpec((1, d, de), lambda e: (e, 0, 0)),
                                            pl.BlockSpec((1, de, d), lambda e: (e, 0, 0))],
        out_specs=pl.BlockSpec((n, d), lambda e: (0, 0)),
        out_shape=jax.ShapeDtypeStruct((n, d), F32),
        scratch_shapes=[pltpu.VMEM((n, d), F32), pltpu.VMEM((n, d), BF16),
                        pltpu.VMEM((n, ROUTE_LANES), F32), pltpu.VMEM((n, d), F32)],
        compiler_params=_cparams(("arbitrary",)),
        name="sample_tail_moe",
    )(*args, w_e1, w_e3, w_e2)


def _t5_bucket_static(dist):
    n = np.maximum(np.asarray(dist, np.int64), 0)
    max_exact = NUM_BUCKETS // 2
    x = np.log(np.maximum(n, 1) / max_exact) / math.log(MAX_DISTANCE / max_exact) * (NUM_BUCKETS - max_exact)
    on_edge = (np.abs(x - np.rint(x)) < 2e-5) & (x > 0.5) & (np.rint(x) < NUM_BUCKETS - max_exact)
    assert not on_edge.any()
    large = np.minimum(max_exact + np.floor(x + 1e-9).astype(np.int64), NUM_BUCKETS - 1)
    return np.where(n < max_exact, n, large)


def _bias_lookup(rel_bias, dist):
    bucket = _t5_bucket_static(dist)
    onehot = (bucket.reshape(-1, 1) == np.arange(NUM_BUCKETS)[None, :]).astype(np.float32)
    table = jnp.einsum('nb,bh->hn', onehot, rel_bias.astype(F32), precision=lax.Precision.HIGHEST)
    return table.reshape((rel_bias.shape[1],) + bucket.shape)


def prompt_bias_table(rel_bias, dilation):
    i = np.arange(ATT_BLK)[:, None]
    j = np.arange(2 * ATT_BLK)[None, :]
    return _bias_lookup(rel_bias, (i + ATT_BLK - j) * dilation)


def sample_bias_tables(rel_bias):
    dist = WIN_MAX - np.arange(WIN_MAX)
    slot_bias = _bias_lookup(rel_bias, dist)
    bias, valid, valid0 = [], [], []
    for window, d in DILATED_CFGS:
        member = (dist % d == 0) & (dist <= window) & ((PAST_LEN - dist) >= 0)
        bias.append(slot_bias)
        valid.append(jnp.asarray(np.broadcast_to(member.astype(np.float32)[None, :], (N_HEADS, WIN_MAX))))
        valid0.append(jnp.full((N_HEADS, LANES), float(PAST_LEN >= 0), F32))
    bias0 = jnp.broadcast_to(_bias_lookup(rel_bias, np.zeros((1,), np.int64)), (N_HEADS, LANES))
    return jnp.stack(bias), jnp.stack(valid), bias0, jnp.stack(valid0)


def kernel(x_prompt, x_sample, cache_k, cache_v, state_conv, c_prompt, c_sample, rel_bias, w_ada, b_ada, g_norm1,
           w_in, conv_w, conv_b, ln_g, ln_b, w_out, g_norm2, w_router_group, b_router_group, w_router_expert,
           b_router_expert, w_expert_gate, w_expert_up, w_expert_down, g_final):
    depth = w_ada.shape[0]
    assert depth == 1, "single-layer step"
    bp, s, d = x_prompt.shape
    ns = x_sample.shape[0]
    w_in_b = w_in[0].astype(BF16)
    w_out_b = w_out[0].astype(BF16)
    wr_hi, wr_lo, br = router_weights(w_router_group[0], b_router_group[0], w_router_expert[0], b_router_expert[0])

    pad = (-(bp + ns)) % 8
    c_all = jnp.concatenate([c_prompt, c_sample, jnp.zeros((pad, d), c_prompt.dtype)], axis=0)
    mod = ada_modulation(c_all, w_ada[0], b_ada[0])
    mod_p = mod[:bp].reshape(bp, 1, 6 * d)
    sh1, sc1, gt1, sh2, sc2, gt2 = [mod_p[..., j * d:(j + 1) * d] for j in range(6)]
    mod_s = mod[bp:bp + ns]
    ssh1, ssc1, sgt1, ssh2, ssc2, sgt2 = [mod_s[:, j * d:(j + 1) * d] for j in range(6)]

    qs, ks, vs, (kc, vc, conv_act, tail) = prompt_inproj(x_prompt, sc1, sh1, g_norm1[0], w_in_b, conv_w[0], conv_b[0],
                                                         ln_g[0], ln_b[0])
    o_list, lse_list = [], []
    for (window, dil), q, k, v in zip(DILATED_CFGS, qs, ks, vs):
        o, lse = attention_branch(q, k, v, prompt_bias_table(rel_bias, dil), window, dil)
        o_list.append(o)
        lse_list.append(lse)
    x1, h2, route, counts = prompt_outproj(o_list, lse_list, conv_act, x_prompt, gt1, sc2, sh2, g_norm2[0], w_out_b,
                                           wr_hi, wr_lo, br)
    t = bp * s
    nblocks = (2 * t) // MOE_BLK + N_EXPERTS
    assert d == ROW_TILE * LANES
    dest, blk_e, nused, pend_ext = slot_plan(route, counts, nblocks)
    xs = moe_dispatch(h2.reshape(t, d), dest, pend_ext, nblocks * MOE_BLK)
    yb = moe_experts(xs, blk_e, nused, w_expert_gate[0], w_expert_up[0], w_expert_down[0])
    y_prompt = moe_combine(yb, dest, x1, route, gt2, g_final)

    xs2 = x_sample.reshape(ns, d)
    to_feature_major = lambda c: jnp.transpose(c, (0, 2, 3, 1)).reshape(c.shape[0], D_ATT, WIN_MAX)
    from_feature_major = lambda c: jnp.transpose(c.reshape(-1, N_HEADS, HEAD_DIM, WIN_MAX), (0, 3, 1, 2))[None]
    sq, sk, sv, sglu = sample_inproj(xs2, ssc1, ssh1, g_norm1[0], w_in_b)
    ko, vo, att_s = sample_cache_attention(sq, sk, sv, to_feature_major(cache_k[0]), to_feature_major(cache_v[0]),
                                           *sample_bias_tables(rel_bias))
    st_taps = jnp.transpose(state_conv[0], (1, 0, 2))
    y_s = sample_tail(att_s, sglu, st_taps, xs2, sgt1, ssc2, ssh2, sgt2, conv_w[0], conv_b[0], ln_g[0], ln_b[0],
                      w_out_b, g_norm2[0], wr_hi, wr_lo, br, g_final, w_expert_gate[0], w_expert_up[0],
                      w_expert_down[0])
    new_conv_s = jnp.transpose(jnp.concatenate([st_taps[1:], sglu[None]], axis=0), (1, 0, 2))

    return (y_prompt, y_s.reshape(ns, 1, d),
            from_feature_major(kc), from_feature_major(vc), tail[:, CONV_PAD - (CONV_W - 1):][None],
            from_feature_major(ko), from_feature_major(vo), new_conv_s[None])
```

```python
import functools
import math

import numpy as np
import jax
import jax.numpy as jnp
from jax import lax
from jax.experimental import pallas as pl
from jax.experimental.pallas import tpu as pltpu

F32 = jnp.float32
BF16 = jnp.bfloat16

D_MODEL = 1024
HEAD_DIM = 64
N_HEADS = 8
D_ATT = N_HEADS * HEAD_DIM
D_CONV = D_MODEL - D_ATT
DILATED_CFGS = ((128, 1), (512, 4), (2048, 16))
WIN_MAX = max(w for w, _ in DILATED_CFGS)
ATT_BLK = 128
CONV_W = 31
NUM_BUCKETS = 32
MAX_DISTANCE = WIN_MAX
N_GROUPS = 4
EXPERTS_PER_GROUP = 8
N_EXPERTS = N_GROUPS * EXPERTS_PER_GROUP
D_EXPERT = 512
EPS = 1e-6
NEG_INF = -1e30
PAST_LEN = 8192

LANES = 128
ROW_TILE = 8
CONV_PAD = 32
CONV_TAIL = 16
ROUTE_LANES = 128
GROUP_LANE0 = N_EXPERTS
MOE_BLK = 512
VMEM_LIMIT = 56 * 1024 * 1024


def _cparams(sem):
    return pltpu.CompilerParams(dimension_semantics=sem, vmem_limit_bytes=VMEM_LIMIT)


def _sigmoid(x):
    return 1.0 / (1.0 + jnp.exp(-x))


def _silu(x):
    return x * _sigmoid(x)


def _rms_scale(x):
    return x * lax.rsqrt(jnp.mean(x * x, axis=-1, keepdims=True) + EPS)


def _dot(a, b):
    return jnp.dot(a, b, preferred_element_type=F32)


def _ada_body(c_ref, w_ref, b_ref, o_ref):
    s = _silu(c_ref[...]).astype(BF16)
    o_ref[...] = _dot(s, w_ref[...].astype(BF16)) + b_ref[...]


def ada_modulation(c, w_ada, b_ada):
    rows, d = c.shape
    n = w_ada.shape[1]
    tn = 1024
    return pl.pallas_call(
        _ada_body,
        grid=(n // tn,),
        in_specs=[pl.BlockSpec((rows, d), lambda j: (0, 0)),
                  pl.BlockSpec((d, tn), lambda j: (0, j)),
                  pl.BlockSpec((1, tn), lambda j: (0, j))],
        out_specs=pl.BlockSpec((rows, tn), lambda j: (0, j)),
        out_shape=jax.ShapeDtypeStruct((rows, n), F32),
        compiler_params=_cparams(("arbitrary",)),
        name="ada_modulation",
    )(c, w_ada, b_ada.reshape(1, n))


def _conv_ln_swish(cbuf, tm, cw_ref, cb_ref, lg_ref, lb_ref, out_ref):
    chunk = 128
    base0 = CONV_PAD - (CONV_W - 1)
    for c in range(tm // chunk):
        pieces = []
        for lt in range(D_CONV // LANES):
            ls = slice(lt * LANES, (lt + 1) * LANES)
            acc = jnp.broadcast_to(cb_ref[:, ls], (chunk, LANES))
            for phase in range(ROW_TILE):
                part = None
                for j in range(CONV_W):
                    if (base0 + j) % ROW_TILE != phase:
                        continue
                    r0 = c * chunk + base0 + j - phase
                    term = cw_ref[j:j + 1, ls] * cbuf[r0:r0 + chunk + ROW_TILE, ls]
                    part = term if part is None else part + term
                if part is not None:
                    acc = acc + part[phase:phase + chunk, :]
            pieces.append(acc)
        acc = jnp.concatenate(pieces, axis=-1)
        mu = jnp.mean(acc, axis=-1, keepdims=True)
        xc = acc - mu
        var = jnp.mean(xc * xc, axis=-1, keepdims=True)
        y = xc * lax.rsqrt(var + EPS) * lg_ref[...] + lb_ref[...]
        out_ref[0, c * chunk:(c + 1) * chunk, :] = _silu(y).astype(out_ref.dtype)


def _lane_slabs_store(slab, rows):
    for c in range(rows.shape[1] // LANES):
        slab[c] = rows[:, c * LANES:(c + 1) * LANES]


def _store_residue_major(slab_a, slab_b, rows, out_refs):
    tm = rows.shape[0]
    nslab = slab_a.shape[0]
    _lane_slabs_store(slab_a, rows)
    src, prev_d = slab_a, 1
    dilations = [d for _, d in DILATED_CFGS]
    assert dilations[0] == 1 and len(dilations) <= 3
    for d, ref in zip(dilations, out_refs):
        if d == 1:
            ref[0] = rows.astype(ref.dtype)
            continue
        ratio, n = d // prev_d, tm // d
        for r in range(d):
            r_prev, j = r % prev_d, r // prev_d
            base = r_prev * (tm // prev_d) + j
            piece = jnp.concatenate([src[c, pl.ds(base, n, stride=ratio), :] for c in range(nslab)], axis=-1)
            ref[0, r] = piece.astype(ref.dtype)
            if d != dilations[-1]:
                for c in range(nslab):
                    slab_b[c, r * n:(r + 1) * n, :] = piece[:, c * LANES:(c + 1) * LANES]
        src, prev_d = slab_b, d


def _inproj_body(x_ref, sc_ref, sh_ref, g_ref, w_ref, cw_ref, cb_ref, lg_ref, lb_ref,
                 q1, q4, q16, k1, k4, k16, v1, v4, v16, kc_ref, vc_ref, conv_ref, tail_ref, cbuf, slab, slab_b):
    i = pl.program_id(1)
    tm = x_ref.shape[1]

    @pl.when(i == 0)
    def _():
        cbuf[0:CONV_PAD, :] = jnp.zeros((CONV_PAD, D_CONV), F32)
        cbuf[CONV_PAD + tm:, :] = jnp.zeros((CONV_TAIL, D_CONV), F32)

    x = x_ref[0]
    h = (_rms_scale(x) * g_ref[...] * (1.0 + sc_ref[0]) + sh_ref[0]).astype(BF16)

    def col(j):
        return _dot(h, w_ref[:, j * D_ATT:(j + 1) * D_ATT])

    _store_residue_major(slab, slab_b, col(0) * (HEAD_DIM ** -0.5), (q1, q4, q16))
    k = col(1)
    _store_residue_major(slab, slab_b, k, (k1, k4, k16))
    v = col(2)
    _store_residue_major(slab, slab_b, v, (v1, v4, v16))

    kc_ref[0] = k.T
    vc_ref[0] = v.T

    glu = col(3) * _sigmoid(col(4))
    cbuf[CONV_PAD:CONV_PAD + tm, :] = glu
    tail_ref[0] = glu[tm - CONV_PAD:, :]
    _conv_ln_swish(cbuf, tm, cw_ref, cb_ref, lg_ref, lb_ref, conv_ref)
    cbuf[0:CONV_PAD, :] = cbuf[tm:tm + CONV_PAD, :]


def prompt_inproj(x, sc1, sh1, g1, w_in_bf16, conv_w, conv_b, ln_g, ln_b, tm=512):
    b, s, d = x.shape
    nt = s // tm
    first_cache_tile = (s - WIN_MAX) // tm
    row = lambda v: v.reshape(1, -1)
    tok = lambda bi, i: (bi, i, 0)
    per_b = lambda bi, i: (bi, 0, 0)
    cst = lambda bi, i: (0, 0)
    cache = lambda bi, i: (bi, 0, jnp.maximum(i - first_cache_tile, 0))
    qkv_specs, qkv_shapes = [], []
    for _ in range(3):
        for _, dil in DILATED_CFGS:
            if dil == 1:
                qkv_specs.append(pl.BlockSpec((1, tm, D_ATT), tok))
                qkv_shapes.append(jax.ShapeDtypeStruct((b, s, D_ATT), BF16))
            else:
                qkv_specs.append(pl.BlockSpec((1, dil, tm // dil, D_ATT), lambda bi, i: (bi, 0, i, 0)))
                qkv_shapes.append(jax.ShapeDtypeStruct((b, dil, s // dil, D_ATT), BF16))
    outs = pl.pallas_call(
        _inproj_body,
        grid=(b, nt),
        in_specs=[pl.BlockSpec((1, tm, d), tok),
                  pl.BlockSpec((1, 1, d), per_b),
                  pl.BlockSpec((1, 1, d), per_b),
                  pl.BlockSpec((1, d), cst),
                  pl.BlockSpec(w_in_bf16.shape, cst),
                  pl.BlockSpec(conv_w.shape, cst),
                  pl.BlockSpec((1, D_CONV), cst),
                  pl.BlockSpec((1, D_CONV), cst),
                  pl.BlockSpec((1, D_CONV), cst)],
        out_specs=qkv_specs + [pl.BlockSpec((1, D_ATT, tm), cache),
                               pl.BlockSpec((1, D_ATT, tm), cache),
                               pl.BlockSpec((1, tm, D_CONV), tok),
                               pl.BlockSpec((1, CONV_PAD, D_CONV), per_b)],
        out_shape=qkv_shapes + [jax.ShapeDtypeStruct((b, D_ATT, WIN_MAX), F32),
                                jax.ShapeDtypeStruct((b, D_ATT, WIN_MAX), F32),
                                jax.ShapeDtypeStruct((b, s, D_CONV), BF16),
                                jax.ShapeDtypeStruct((b, CONV_PAD, D_CONV), F32)],
        scratch_shapes=[pltpu.VMEM((tm + CONV_PAD + CONV_TAIL, D_CONV), F32),
                        pltpu.VMEM((D_ATT // LANES, tm, LANES), F32),
                        pltpu.VMEM((D_ATT // LANES, tm, LANES), F32)],
        compiler_params=_cparams(("arbitrary", "arbitrary")),
        name="prompt_inproj_conv",
    )(x, sc1, sh1, row(g1), w_in_bf16, conv_w, row(conv_b), row(ln_g), row(ln_b))
    nd = len(DILATED_CFGS)
    return outs[0:nd], outs[nd:2 * nd], outs[2 * nd:3 * nd], outs[3 * nd:]


def _attn_body(q_ref, kc_ref, kp_ref, vc_ref, vp_ref, bias_ref, hmask_ref, o_ref, lse_ref, *, wsub):
    i = pl.program_id(2)
    nq = q_ref.shape[0] // ATT_BLK
    npair = N_HEADS // 2
    qi = lax.broadcasted_iota(jnp.int32, (ATT_BLK, 2 * ATT_BLK), 0)
    kj = lax.broadcasted_iota(jnp.int32, (ATT_BLK, 2 * ATT_BLK), 1)
    rel = qi + ATT_BLK - kj
    band = (rel >= 0) & (rel <= wsub)
    band_first = band & ((kj >= ATT_BLK) | (i > 0))
    low = lax.broadcasted_iota(jnp.int32, (ATT_BLK, LANES), 1) < HEAD_DIM
    for sub in range(nq):
        r0 = sub * ATT_BLK
        valid = band_first if sub == 0 else band
        scores, values = [], []
        for hp in range(npair):
            cs = slice(hp * LANES, (hp + 1) * LANES)
            qp = q_ref[r0:r0 + ATT_BLK, cs]
            if sub == 0:
                kk = jnp.concatenate([kp_ref[:, cs], kc_ref[0:ATT_BLK, cs]], axis=0)
                vv = jnp.concatenate([vp_ref[:, cs], vc_ref[0:ATT_BLK, cs]], axis=0)
            else:
                kk = kc_ref[r0 - ATT_BLK:r0 + ATT_BLK, cs]
                vv = vc_ref[r0 - ATT_BLK:r0 + ATT_BLK, cs]
            values.append(vv)
            for half in range(2):
                s = lax.dot_general(qp * hmask_ref[half], kk, (((1,), (1,)), ((), ())), preferred_element_type=F32)
                scores.append(jnp.where(valid, s + bias_ref[2 * hp + half], NEG_INF))
        s = jnp.concatenate(scores, axis=0)
        m = jnp.max(s, axis=-1, keepdims=True)
        p = jnp.exp(s - m)
        l = jnp.sum(p, axis=-1, keepdims=True)
        p16 = p.astype(BF16)
        inv = 1.0 / l
        lse = m + jnp.log(l)
        for hp in range(npair):
            cs = slice(hp * LANES, (hp + 1) * LANES)
            ra = slice(2 * hp * ATT_BLK, (2 * hp + 1) * ATT_BLK)
            rb = slice((2 * hp + 1) * ATT_BLK, (2 * hp + 2) * ATT_BLK)
            oa = _dot(p16[ra], values[hp]) * inv[ra]
            ob = _dot(p16[rb], values[hp]) * inv[rb]
            o_ref[r0:r0 + ATT_BLK, cs] = jnp.where(low, oa, ob).astype(o_ref.dtype)
            lse_ref[r0:r0 + ATT_BLK, cs] = jnp.where(low, lse[ra], lse[rb])


def attention_branch(q, k, v, bias, window, dilation, qblocks=4):
    c = q.shape[-1]
    b, n = q.shape[0], q.shape[-2]
    wsub = window // dilation
    tq = ATT_BLK * qblocks
    if dilation == 1:
        cur_spec = pl.BlockSpec((None, tq, c), lambda bi, r, i: (bi, i, 0))
        prev_spec = pl.BlockSpec((None, ATT_BLK, c), lambda bi, r, i: (bi, jnp.maximum(i * qblocks - 1, 0), 0))
    else:
        cur_spec = pl.BlockSpec((None, None, tq, c), lambda bi, r, i: (bi, r, i, 0))
        prev_spec = pl.BlockSpec((None, None, ATT_BLK, c),
                                 lambda bi, r, i: (bi, r, jnp.maximum(i * qblocks - 1, 0), 0))
    lane = jnp.arange(LANES)[None, :]
    hmask = jnp.broadcast_to(jnp.stack([lane < HEAD_DIM, lane >= HEAD_DIM]).astype(BF16), (2, ATT_BLK, LANES))
    return pl.pallas_call(
        functools.partial(_attn_body, wsub=wsub),
        grid=(b, dilation, n // tq),
        in_specs=[cur_spec, cur_spec, prev_spec, cur_spec, prev_spec,
                  pl.BlockSpec(bias.shape, lambda bi, r, i: (0, 0, 0)),
                  pl.BlockSpec(hmask.shape, lambda bi, r, i: (0, 0, 0))],
        out_specs=[cur_spec, cur_spec],
        out_shape=[jax.ShapeDtypeStruct(q.shape, BF16), jax.ShapeDtypeStruct(q.shape, F32)],
        compiler_params=_cparams(("arbitrary", "arbitrary", "arbitrary")),
        name=f"attn_branch_d{dilation}",
    )(q, k, k, v, v, bias, hmask)


def _split_bf16(a):
    hi = a.astype(BF16)
    lo = (a - hi.astype(F32)).astype(BF16)
    return hi, lo


def _router_logits(h2, wr_hi_ref, wr_lo_ref, br_ref):
    hi, lo = _split_bf16(h2)
    return _dot(hi, wr_hi_ref[...]) + (_dot(hi, wr_lo_ref[...]) + _dot(lo, wr_hi_ref[...])) + br_ref[...]


def _route(logits):
    rows = logits.shape[0]
    lane = lax.broadcasted_iota(jnp.int32, (rows, ROUTE_LANES), 1)
    lanef = lane.astype(F32)
    big = float(ROUTE_LANES)
    is_g = (lane >= GROUP_LANE0) & (lane < GROUP_LANE0 + N_GROUPS)
    glog = jnp.where(is_g, logits, -jnp.inf)
    gmax = jnp.max(glog, axis=-1, keepdims=True)
    gsum = jnp.sum(jnp.exp(glog - gmax), axis=-1, keepdims=True)
    p_grp = 1.0 / gsum
    glane = jnp.min(jnp.where(glog == gmax, lanef, big), axis=-1, keepdims=True)
    grp = glane - float(GROUP_LANE0)
    in_grp = jnp.floor(lanef * (1.0 / EXPERTS_PER_GROUP)) == grp
    elog = jnp.where(in_grp, logits, -jnp.inf)
    emax = jnp.max(elog, axis=-1, keepdims=True)
    eexp = jnp.exp(elog - emax)
    pe = eexp / jnp.sum(eexp, axis=-1, keepdims=True)
    pe = jnp.where(in_grp, pe, -1.0)
    p1 = jnp.max(pe, axis=-1, keepdims=True)
    e1 = jnp.min(jnp.where(pe == p1, lanef, big), axis=-1, keepdims=True)
    pe2 = jnp.where(lanef == e1, -1.0, pe)
    p2 = jnp.max(pe2, axis=-1, keepdims=True)
    e2 = jnp.min(jnp.where(pe2 == p2, lanef, big), axis=-1, keepdims=True)
    psum = p1 + p2
    return e1, e2, p_grp * p1 / psum, p_grp * p2 / psum


def _position_order(ref, slab, dilation):
    if dilation == 1:
        return ref[0].astype(F32)
    n = ref.shape[2]
    for r in range(dilation):
        piece = ref[0, r].astype(F32)
        for c in range(slab.shape[0]):
            slab[c, pl.ds(r, n, stride=dilation), :] = piece[:, c * LANES:(c + 1) * LANES]
    return jnp.concatenate([slab[c] for c in range(slab.shape[0])], axis=-1)


def _mix_branches(o_refs, lse_refs, slabs):
    slabs = iter(slabs)
    lses = [_position_order(r, None if d == 1 else next(slabs), d) for r, (_, d) in zip(lse_refs, DILATED_CFGS)]
    outs = [_position_order(r, None if d == 1 else next(slabs), d) for r, (_, d) in zip(o_refs, DILATED_CFGS)]
    mx = jnp.maximum(jnp.maximum(lses[0], lses[1]), lses[2])
    ws = [jnp.exp(l - mx) for l in lses]
    numer = sum(w * o for w, o in zip(ws, outs))
    return numer / (ws[0] + ws[1] + ws[2])


def _outproj_body(o1, o4, o16, l1, l4, l16, conv_ref, x_ref, gt_ref, sc_ref, sh_ref, g_ref,
                  wo_ref, wrh_ref, wrl_ref, br_ref,
                  x1_ref, h2_ref, route_ref, cnt_ref, carry, *slabs):
    first = (pl.program_id(0) == 0) & (pl.program_id(1) == 0)

    @pl.when(first)
    def _():
        carry[...] = jnp.zeros_like(carry)

    tm = x_ref.shape[1]
    att = _mix_branches((o1, o4, o16), (l1, l4, l16), slabs).astype(BF16)
    mix = _dot(att, wo_ref[0:D_ATT, :]) + _dot(conv_ref[0], wo_ref[D_ATT:, :])
    x1 = x_ref[0] + gt_ref[0] * mix
    x1_ref[0] = x1
    h2 = _rms_scale(x1) * g_ref[...] * (1.0 + sc_ref[0]) + sh_ref[0]
    h2_ref[0] = h2.astype(BF16)
    e1, e2, w1, w2 = _route(_router_logits(h2, wrh_ref, wrl_ref, br_ref))

    lane = lax.broadcasted_iota(jnp.int32, (tm, ROUTE_LANES), 1).astype(F32)
    oh1 = lane == e1
    oh2 = lane == e2
    onehot = jnp.where(oh1 | oh2, 1.0, 0.0)
    ri = lax.broadcasted_iota(jnp.int32, (tm, tm), 0)
    ci = lax.broadcasted_iota(jnp.int32, (tm, tm), 1)
    tri = jnp.where(ci < ri, 1.0, 0.0).astype(BF16)
    before = _dot(tri, onehot.astype(BF16)) + carry[...]
    r1 = jnp.sum(jnp.where(oh1, before, 0.0), axis=-1, keepdims=True)
    r2 = jnp.sum(jnp.where(oh2, before, 0.0), axis=-1, keepdims=True)
    carry[...] = carry[...] + jnp.sum(onehot, axis=0, keepdims=True)
    cnt_ref[...] = carry[...]
    rec = jnp.where(lane == 0.0, e1, 0.0)
    for idx, val in ((1, e2), (2, w1), (3, w2), (4, r1), (5, r2)):
        rec = jnp.where(lane == float(idx), val, rec)
    route_ref[0] = rec


def prompt_outproj(o_list, lse_list, conv_act, x, gt1, sc2, sh2, g2, w_out_bf16, wr_hi, wr_lo, br, tm=512):
    b, s, d = x.shape
    tok = lambda bi, i: (bi, i, 0)
    per_b = lambda bi, i: (bi, 0, 0)
    cst = lambda bi, i: (0, 0)
    half = pl.BlockSpec((1, tm, D_ATT), tok)
    branch_specs = [half if dil == 1 else pl.BlockSpec((1, dil, tm // dil, D_ATT), lambda bi, i: (bi, 0, i, 0))
                    for _, dil in DILATED_CFGS]
    n_slabs = 2 * sum(1 for _, dil in DILATED_CFGS if dil > 1)
    return pl.pallas_call(
        _outproj_body,
        grid=(b, s // tm),
        in_specs=branch_specs * 2 + [half, pl.BlockSpec((1, tm, d), tok),
                               pl.BlockSpec((1, 1, d), per_b),
                               pl.BlockSpec((1, 1, d), per_b),
                               pl.BlockSpec((1, 1, d), per_b),
                               pl.BlockSpec((1, d), cst),
                               pl.BlockSpec(w_out_bf16.shape, cst),
                               pl.BlockSpec(wr_hi.shape, cst),
                               pl.BlockSpec(wr_lo.shape, cst),
                               pl.BlockSpec((1, ROUTE_LANES), cst)],
        out_specs=[pl.BlockSpec((1, tm, d), tok),
                   pl.BlockSpec((1, tm, d), tok),
                   pl.BlockSpec((1, tm, ROUTE_LANES), tok),
                   pl.BlockSpec((1, ROUTE_LANES), cst)],
        out_shape=[jax.ShapeDtypeStruct((b, s, d), F32),
                   jax.ShapeDtypeStruct((b, s, d), BF16),
                   jax.ShapeDtypeStruct((b, s, ROUTE_LANES), F32),
                   jax.ShapeDtypeStruct((1, ROUTE_LANES), F32)],
        scratch_shapes=[pltpu.VMEM((1, ROUTE_LANES), F32)] + [pltpu.VMEM((D_ATT // LANES, tm, LANES), F32)] * n_slabs,
        compiler_params=_cparams(("arbitrary", "arbitrary")),
        name="prompt_outproj_router",
    )(*o_list, *lse_list, conv_act, x, gt1, sc2, sh2, g2.reshape(1, d), w_out_bf16, wr_hi, wr_lo, br)


def router_weights(w_rg, b_rg, w_re, b_re):
    d = w_rg.shape[0]
    fill = ROUTE_LANES - N_EXPERTS - N_GROUPS
    w = jnp.concatenate([w_re.astype(F32), w_rg.astype(F32), jnp.zeros((d, fill), F32)], axis=1)
    bias = jnp.concatenate([b_re.astype(F32), b_rg.astype(F32), jnp.zeros((fill,), F32)])[None, :]
    hi = w.astype(BF16)
    lo = (w - hi.astype(F32)).astype(BF16)
    return hi, lo, bias


def _rows_from_tiles(tile_ref, n):
    return jnp.concatenate([tile_ref[pl.ds(c, n, stride=ROW_TILE), :] for c in range(ROW_TILE)], axis=-1)


def _rows_to_tiles(tile_ref, rows):
    n = rows.shape[0]
    for c in range(ROW_TILE):
        tile_ref[pl.ds(c, n, stride=ROW_TILE), :] = rows[:, c * LANES:(c + 1) * LANES]


def _dispatch_body(dest_ref, prev_ref, pend_ref, h2_ref, xs_hbm, tiles, zbuf, sems, zsem, *, chunk, nsteps):
    step = pl.program_id(0)

    @pl.when(step == 0)
    def _():
        zbuf[...] = jnp.zeros_like(zbuf)

        def zcopy(e):
            start = pl.multiple_of(jnp.maximum(pend_ref[e + 1] - MOE_BLK * ROW_TILE, 0), ROW_TILE)
            return pltpu.make_async_copy(zbuf, xs_hbm.at[pl.ds(start, MOE_BLK * ROW_TILE)], zsem)

        def zstart(e, c):
            @pl.when(pend_ref[e + 1] > pend_ref[e])
            def _():
                zcopy(e).start()
            return c

        def zwait(e, c):
            @pl.when(pend_ref[e + 1] > pend_ref[e])
            def _():
                zcopy(e).wait()
            return c

        lax.fori_loop(0, N_EXPERTS, zstart, 0)
        lax.fori_loop(0, N_EXPERTS, zwait, 0)

        blk_rows = MOE_BLK * ROW_TILE
        nblk = xs_hbm.shape[0] // blk_rows

        def tcopy(i):
            return pltpu.make_async_copy(zbuf, xs_hbm.at[pl.ds(pl.multiple_of(i * blk_rows, blk_rows), blk_rows)],
                                         zsem)

        def tstart(i, c):
            @pl.when(i * blk_rows >= pend_ref[N_EXPERTS])
            def _():
                tcopy(i).start()
            return c

        def twait(i, c):
            @pl.when(i * blk_rows >= pend_ref[N_EXPERTS])
            def _():
                tcopy(i).wait()
            return c

        lax.fori_loop(0, nblk, tstart, 0)
        lax.fori_loop(0, nblk, twait, 0)

    def copy(dref, slot, t, k):
        src = pl.multiple_of(t * ROW_TILE, ROW_TILE)
        dst = pl.multiple_of(dref[0, k, t], ROW_TILE)
        return pltpu.make_async_copy(tiles.at[slot, pl.ds(src, ROW_TILE)], xs_hbm.at[pl.ds(dst, ROW_TILE)],
                                     sems.at[slot])

    def issue(dref, slot):
        def body(t, c):
            copy(dref, slot, t, 0).start(priority=0)
            copy(dref, slot, t, 1).start(priority=1)
            return c
        lax.fori_loop(0, chunk, body, 0, unroll=8)

    def drain(dref, slot):
        def body(t, c):
            copy(dref, slot, t, 0).wait()
            copy(dref, slot, t, 1).wait()
            return c
        lax.fori_loop(0, chunk, body, 0, unroll=8)

    for slot in range(2):
        @pl.when(lax.rem(step, 2) == slot)
        def _(slot=slot):
            _rows_to_tiles(tiles.at[slot], h2_ref[...].astype(F32))
            issue(dest_ref, slot)

            @pl.when(step > 0)
            def _():
                drain(prev_ref, 1 - slot)

            @pl.when(step == nsteps - 1)
            def _():
                drain(dest_ref, slot)


def moe_dispatch(h2_flat, dest_rt, pends_rt, nslots, chunk=512):
    t, d = h2_flat.shape
    assert d == ROW_TILE * LANES
    nsteps = t // chunk
    dest3 = jnp.transpose(dest_rt.reshape(2, nsteps, chunk), (1, 0, 2))
    return pl.pallas_call(
        functools.partial(_dispatch_body, chunk=chunk, nsteps=nsteps),
        grid=(nsteps,),
        in_specs=[pl.BlockSpec((1, 2, chunk), lambda i: (i, 0, 0), memory_space=pltpu.SMEM),
                  pl.BlockSpec((1, 2, chunk), lambda i: (jnp.maximum(i - 1, 0), 0, 0), memory_space=pltpu.SMEM),
                  pl.BlockSpec(memory_space=pltpu.SMEM),
                  pl.BlockSpec((chunk, d), lambda i: (i, 0))],
        out_specs=pl.BlockSpec(memory_space=pl.ANY),
        out_shape=jax.ShapeDtypeStruct((nslots * ROW_TILE, LANES), F32),
        scratch_shapes=[pltpu.VMEM((2, chunk * ROW_TILE, LANES), F32),
                        pltpu.VMEM((MOE_BLK * ROW_TILE, LANES), F32),
                        pltpu.SemaphoreType.DMA((2,)),
                        pltpu.SemaphoreType.DMA(())],
        compiler_params=_cparams(("arbitrary",)),
        name="moe_dispatch",
    )(dest3, dest3, pends_rt, h2_flat)


def _experts_body(blk_e_ref, nused_ref, xs_ref, w1_ref, w3_ref, w2_ref, y_ref, w1b, w3b, w2b):
    i = pl.program_id(0)
    new_expert = (i == 0) | (blk_e_ref[i] != blk_e_ref[jnp.maximum(i - 1, 0)])

    @pl.when((i < nused_ref[0]) & new_expert)
    def _():
        w1b[...] = w1_ref[0].astype(BF16)
        w3b[...] = w3_ref[0].astype(BF16)
        w2b[...] = w2_ref[0].astype(BF16)

    @pl.when(i < nused_ref[0])
    def _():
        xb = _rows_from_tiles(xs_ref, MOE_BLK).astype(BF16)
        a = (_silu(_dot(xb, w1b[...])) * _dot(xb, w3b[...])).astype(BF16)
        _rows_to_tiles(y_ref, _dot(a, w2b[...]))

    @pl.when(i >= nused_ref[0])
    def _():
        y_ref[...] = jnp.zeros_like(y_ref)


def moe_experts(xs, blk_e, nused, w_e1, w_e3, w_e2):
    nb = xs.shape[0] // (MOE_BLK * ROW_TILE)
    _, d, de = w_e1.shape
    blk = lambda i, be, nu: (jnp.minimum(i, nu[0] - 1), 0)
    wsel = lambda i, be, nu: (be[i], 0, 0)
    grid_spec = pltpu.PrefetchScalarGridSpec(
        num_scalar_prefetch=2,
        grid=(nb,),
        in_specs=[pl.BlockSpec((MOE_BLK * ROW_TILE, LANES), blk),
                  pl.BlockSpec((1, d, de), wsel),
                  pl.BlockSpec((1, d, de), wsel),
                  pl.BlockSpec((1, de, d), wsel)],
        out_specs=pl.BlockSpec((MOE_BLK * ROW_TILE, LANES), lambda i, be, nu: (i, 0)),
        scratch_shapes=[pltpu.VMEM((d, de), BF16), pltpu.VMEM((d, de), BF16), pltpu.VMEM((de, d), BF16)],
    )
    return pl.pallas_call(
        _experts_body,
        grid_spec=grid_spec,
        out_shape=jax.ShapeDtypeStruct(xs.shape, F32),
        compiler_params=_cparams(("arbitrary",)),
        name="moe_experts",
    )(blk_e, nused, xs, w_e1, w_e3, w_e2)


def _combine_body(dest_ref, next_ref, yb_hbm, x1_ref, route_ref, gt_ref, gf_ref, out_ref, ya0, ya1, sems,
                  *, tm, nsteps):
    step = pl.program_id(0) * pl.num_programs(1) + pl.program_id(1)

    def copy(dref, slot, t, k):
        src = pl.multiple_of(dref[0, k, t], ROW_TILE)
        dst = pl.multiple_of(t * ROW_TILE, ROW_TILE)
        buf = ya0 if k == 0 else ya1
        return pltpu.make_async_copy(yb_hbm.at[pl.ds(src, ROW_TILE)], buf.at[slot, pl.ds(dst, ROW_TILE)],
                                     sems.at[slot])

    def issue(dref, slot):
        def body(t, c):
            copy(dref, slot, t, 0).start(priority=0)
            copy(dref, slot, t, 1).start(priority=1)
            return c
        lax.fori_loop(0, tm, body, 0, unroll=8)

    def drain(dref, slot):
        def body(t, c):
            copy(dref, slot, t, 0).wait()
            copy(dref, slot, t, 1).wait()
            return c
        lax.fori_loop(0, tm, body, 0, unroll=8)

    for slot in range(2):
        @pl.when(lax.rem(step, 2) == slot)
        def _(slot=slot):
            @pl.when(step == 0)
            def _():
                issue(dest_ref, slot)

            @pl.when(step + 1 < nsteps)
            def _():
                issue(next_ref, 1 - slot)

            drain(dest_ref, slot)
            rec = route_ref[0]
            moe = (rec[:, 2:3] * _rows_from_tiles(ya0.at[slot], tm)
                   + rec[:, 3:4] * _rows_from_tiles(ya1.at[slot], tm))
            y = x1_ref[0] + gt_ref[0] * moe
            out_ref[0] = _rms_scale(y) * gf_ref[...]


def moe_combine(yb, dest_rt, x1, route, gt2, g_final, tm=256):
    b, s, d = x1.shape
    nt = s // tm
    nsteps = b * nt
    dest3 = jnp.transpose(dest_rt.reshape(2, nsteps, tm), (1, 0, 2))
    tok = lambda bi, i: (bi, i, 0)
    return pl.pallas_call(
        functools.partial(_combine_body, tm=tm, nsteps=nsteps),
        grid=(b, nt),
        in_specs=[pl.BlockSpec((1, 2, tm), lambda bi, i: (bi * nt + i, 0, 0), memory_space=pltpu.SMEM),
                  pl.BlockSpec((1, 2, tm), lambda bi, i: (jnp.minimum(bi * nt + i + 1, nsteps - 1), 0, 0),
                               memory_space=pltpu.SMEM),
                  pl.BlockSpec(memory_space=pl.ANY),
                  pl.BlockSpec((1, tm, d), tok),
                  pl.BlockSpec((1, tm, ROUTE_LANES), tok),
                  pl.BlockSpec((1, 1, d), lambda bi, i: (bi, 0, 0)),
                  pl.BlockSpec((1, d), lambda bi, i: (0, 0))],
        out_specs=pl.BlockSpec((1, tm, d), tok),
        out_shape=jax.ShapeDtypeStruct((b, s, d), F32),
        scratch_shapes=[pltpu.VMEM((2, tm * ROW_TILE, LANES), F32), pltpu.VMEM((2, tm * ROW_TILE, LANES), F32),
                        pltpu.SemaphoreType.DMA((2,))],
        compiler_params=_cparams(("arbitrary", "arbitrary")),
        name="moe_combine_final",
    )(dest3, dest3, yb, x1, route, gt2, g_final.reshape(1, d))


def slot_plan(route, counts, nblocks):
    cnt = counts[0, :N_EXPERTS].astype(jnp.int32)
    padded = (cnt + MOE_BLK - 1) // MOE_BLK * MOE_BLK
    pends = jnp.cumsum(padded)
    pstarts = pends - padded
    experts = jnp.arange(N_EXPERTS, dtype=jnp.int32)
    dests = []
    for k in range(2):
        eid = route[..., k].reshape(-1).astype(jnp.int32)
        rank = route[..., 4 + k].reshape(-1).astype(jnp.int32)
        dests.append(jnp.sum(jnp.where(eid[:, None] == experts, pstarts, 0), axis=-1) + rank)
    dest = jnp.stack(dests)
    blk_start = jnp.arange(nblocks, dtype=jnp.int32) * MOE_BLK
    blk_e = jnp.minimum(jnp.sum((pends[None, :] <= blk_start[:, None]).astype(jnp.int32), axis=1), N_EXPERTS - 1)
    nused = (pends[-1:] // MOE_BLK).astype(jnp.int32)
    pend_ext = jnp.concatenate([jnp.zeros((1,), jnp.int32), pends.astype(jnp.int32)])
    return (dest * ROW_TILE).astype(jnp.int32), blk_e, nused, (pend_ext * ROW_TILE).astype(jnp.int32)


def _sample_inproj_body(x_ref, sc_ref, sh_ref, g_ref, w_ref, q_ref, k_ref, v_ref, glu_ref):
    h = (_rms_scale(x_ref[...]) * g_ref[...] * (1.0 + sc_ref[...]) + sh_ref[...]).astype(BF16)

    def col(j):
        return _dot(h, w_ref[:, j * D_ATT:(j + 1) * D_ATT])

    q_ref[...] = col(0) * (HEAD_DIM ** -0.5)
    k_ref[...] = col(1)
    v_ref[...] = col(2)
    glu_ref[...] = col(3) * _sigmoid(col(4))


def sample_inproj(x, sc1, sh1, g1, w_in_bf16):
    n, d = x.shape
    out = jax.ShapeDtypeStruct((n, D_ATT), F32)
    return pl.pallas_call(
        _sample_inproj_body,
        out_shape=[out, out, out, out],
        compiler_params=pltpu.CompilerParams(vmem_limit_bytes=VMEM_LIMIT),
        name="sample_inproj",
    )(x, sc1, sh1, g1.reshape(1, d), w_in_bf16)


def _sample_cache_body(qt_ref, knt_ref, vnt_ref, k_ref, v_ref, bias_ref, valid_ref, bias0_ref, valid0_ref,
                       ko_ref, vo_ref, att_ref, s_s, p_s, n_s):
    b = pl.program_id(0)
    win = k_ref.shape[2]
    sel = lax.broadcasted_iota(jnp.int32, qt_ref.shape, 1) == b

    def column(ref):
        return jnp.sum(jnp.where(sel, ref[...], 0.0), axis=1, keepdims=True)

    q, kn, vn = column(qt_ref), column(knt_ref), column(vnt_ref)
    kt = k_ref[0]
    vt = v_ref[0]
    for h in range(N_HEADS):
        rows = slice(h * HEAD_DIM, (h + 1) * HEAD_DIM)
        s_s[h:h + 1, :] = jnp.sum(kt[rows, :] * q[rows, :], axis=0, keepdims=True)
        n_s[h:h + 1, :] = jnp.broadcast_to(jnp.sum(kn[rows, :] * q[rows, :], axis=0, keepdims=True), (1, LANES))
    s_all = s_s[...]
    s_new = n_s[...][:, 0:1] + bias0_ref[:, 0:1]
    parts = []
    for br in range(len(DILATED_CFGS)):
        s = jnp.where(valid_ref[br] > 0.0, s_all + bias_ref[br], NEG_INF)
        s0 = jnp.where(valid0_ref[br][:, 0:1] > 0.0, s_new, NEG_INF)
        m = jnp.maximum(jnp.max(s, axis=-1, keepdims=True), s0)
        p = jnp.exp(s - m)
        p0 = jnp.exp(s0 - m)
        parts.append((p, p0, m, jnp.sum(p, axis=-1, keepdims=True) + p0))
    mx = jnp.maximum(jnp.maximum(parts[0][2], parts[1][2]), parts[2][2])
    ws = [jnp.exp(m - mx) for _, _, m, _ in parts]
    inv = 1.0 / sum(w * l for w, (_, _, _, l) in zip(ws, parts))
    p_s[...] = sum(w * p for w, (p, _, _, _) in zip(ws, parts)) * inv
    p_new = sum(w * p0 for w, (_, p0, _, _) in zip(ws, parts)) * inv
    cols = []
    for h in range(N_HEADS):
        rows = slice(h * HEAD_DIM, (h + 1) * HEAD_DIM)
        cols.append(jnp.sum(vt[rows, :] * p_s[h:h + 1, :], axis=1, keepdims=True) + p_new[h:h + 1, :] * vn[rows, :])
    att = jnp.concatenate(cols, axis=0)

    @pl.when(b == 0)
    def _():
        att_ref[...] = jnp.zeros_like(att_ref)

    att_ref[...] = jnp.where(sel, att, att_ref[...])
    last = lax.broadcasted_iota(jnp.int32, kt.shape, 1) == win - 1
    ko_ref[0] = jnp.where(last, kn, pltpu.roll(kt, win - 1, 1))
    vo_ref[0] = jnp.where(last, vn, pltpu.roll(vt, win - 1, 1))


def sample_cache_attention(q, k_new, v_new, cache_kt, cache_vt, bias, valid, bias0, valid0):
    n, c, win = cache_kt.shape
    full = lambda a: pl.BlockSpec(a.shape, lambda i: (0,) * a.ndim)
    seq = pl.BlockSpec((1, c, win), lambda i: (i, 0, 0))
    qt, knt, vnt = q.T, k_new.T, v_new.T
    ko, vo, att_t = pl.pallas_call(
        _sample_cache_body,
        grid=(n,),
        in_specs=[full(qt), full(knt), full(vnt), seq, seq, full(bias), full(valid), full(bias0), full(valid0)],
        out_specs=[seq, seq, pl.BlockSpec((c, n), lambda i: (0, 0))],
        out_shape=[jax.ShapeDtypeStruct(cache_kt.shape, F32), jax.ShapeDtypeStruct(cache_vt.shape, F32),
                   jax.ShapeDtypeStruct((c, n), F32)],
        scratch_shapes=[pltpu.VMEM((N_HEADS, win), F32), pltpu.VMEM((N_HEADS, win), F32),
                        pltpu.VMEM((N_HEADS, LANES), F32)],
        compiler_params=_cparams(("arbitrary",)),
        name="sample_cache_attention",
    )(qt, knt, vnt, cache_kt, cache_vt, bias, valid, bias0, valid0)
    return ko, vo, att_t.T


def _sample_tail_body(att_ref, glu_ref, st_ref, x_ref, gt1_ref, sc2_ref, sh2_ref, gt2_ref,
                      cw_ref, cb_ref, lg_ref, lb_ref, wo_ref, g2_ref, wrh_ref, wrl_ref, br_ref, gf_ref,
                      w1_ref, w3_ref, w2_ref, y_ref, x1_s, h2_s, gate_s, acc_s):
    e = pl.program_id(0)
    n = x_ref.shape[0]

    @pl.when(e == 0)
    def _():
        nhist = CONV_W - 1
        conv = cw_ref[nhist:CONV_W, :] * glu_ref[...] + cb_ref[...]
        for j in range(nhist):
            conv = conv + cw_ref[j:j + 1, :] * st_ref[j]
        mu = jnp.mean(conv, axis=-1, keepdims=True)
        xc = conv - mu
        var = jnp.mean(xc * xc, axis=-1, keepdims=True)
        act = _silu(xc * lax.rsqrt(var + EPS) * lg_ref[...] + lb_ref[...]).astype(BF16)
        mix = _dot(att_ref[...].astype(BF16), wo_ref[0:D_ATT, :]) + _dot(act, wo_ref[D_ATT:, :])
        x1 = x_ref[...] + gt1_ref[...] * mix
        x1_s[...] = x1
        h2 = _rms_scale(x1) * g2_ref[...] * (1.0 + sc2_ref[...]) + sh2_ref[...]
        h2_s[...] = h2.astype(BF16)
        e1, e2, w1, w2 = _route(_router_logits(h2, wrh_ref, wrl_ref, br_ref))
        lane = lax.broadcasted_iota(jnp.int32, (n, ROUTE_LANES), 1).astype(F32)
        gate_s[...] = jnp.where(lane == e1, w1, 0.0) + jnp.where(lane == e2, w2, 0.0)
        acc_s[...] = jnp.zeros_like(acc_s)

    hb = h2_s[...]
    a = (_silu(_dot(hb, w1_ref[0].astype(BF16))) * _dot(hb, w3_ref[0].astype(BF16))).astype(BF16)
    ye = _dot(a, w2_ref[0].astype(BF16))
    lane = lax.broadcasted_iota(jnp.int32, (n, ROUTE_LANES), 1)
    g = jnp.sum(jnp.where(lane == e, gate_s[...], 0.0), axis=-1, keepdims=True)
    acc_s[...] = acc_s[...] + g * ye

    @pl.when(e == pl.num_programs(0) - 1)
    def _():
        y = x1_s[...] + gt2_ref[...] * acc_s[...]
        y_ref[...] = _rms_scale(y) * gf_ref[...]


def sample_tail(att, glu, state_conv, x, gt1, sc2, sh2, gt2, conv_w, conv_b, ln_g, ln_b, w_out_bf16, g2,
                wr_hi, wr_lo, br, g_final, w_e1, w_e3, w_e2):
    n, d = x.shape
    de = w_e1.shape[2]
    row = lambda v: v.reshape(1, -1)
    full = lambda a: pl.BlockSpec(a.shape, lambda e: (0,) * a.ndim)
    args = [att, glu, state_conv, x, gt1, sc2, sh2, gt2, conv_w, row(conv_b), row(ln_g), row(ln_b), w_out_bf16,
            row(g2), wr_hi, wr_lo, br, row(g_final)]
    return pl.pallas_call(
        _sample_tail_body,
        grid=(N_EXPERTS,),
        in_specs=[full(a) for a in args] + [pl.BlockSpec((1, d, de), lambda e: (e, 0, 0)),
                                            pl.BlockSpec((1, d, de), lambda e: (e, 0, 0)),
                                            pl.BlockSpec((1, de, d), lambda e: (e, 0, 0))],
        out_specs=pl.BlockSpec((n, d), lambda e: (0, 0)),
        out_shape=jax.ShapeDtypeStruct((n, d), F32),
        scratch_shapes=[pltpu.VMEM((n, d), F32), pltpu.VMEM((n, d), BF16),
                        pltpu.VMEM((n, ROUTE_LANES), F32), pltpu.VMEM((n, d), F32)],
        compiler_params=_cparams(("arbitrary",)),
        name="sample_tail_moe",
    )(*args, w_e1, w_e3, w_e2)


def _t5_bucket_static(dist):
    n = np.maximum(np.asarray(dist, np.int64), 0)
    max_exact = NUM_BUCKETS // 2
    x = np.log(np.maximum(n, 1) / max_exact) / math.log(MAX_DISTANCE / max_exact) * (NUM_BUCKETS - max_exact)
    on_edge = (np.abs(x - np.rint(x)) < 2e-5) & (x > 0.5) & (np.rint(x) < NUM_BUCKETS - max_exact)
    assert not on_edge.any()
    large = np.minimum(max_exact + np.floor(x + 1e-9).astype(np.int64), NUM_BUCKETS - 1)
    return np.where(n < max_exact, n, large)


def _bias_lookup(rel_bias, dist):
    bucket = _t5_bucket_static(dist)
    onehot = (bucket.reshape(-1, 1) == np.arange(NUM_BUCKETS)[None, :]).astype(np.float32)
    table = jnp.einsum('nb,bh->hn', onehot, rel_bias.astype(F32), precision=lax.Precision.HIGHEST)
    return table.reshape((rel_bias.shape[1],) + bucket.shape)


def prompt_bias_table(rel_bias, dilation):
    i = np.arange(ATT_BLK)[:, None]
    j = np.arange(2 * ATT_BLK)[None, :]
    return _bias_lookup(rel_bias, (i + ATT_BLK - j) * dilation)


def sample_bias_tables(rel_bias):
    dist = WIN_MAX - np.arange(WIN_MAX)
    slot_bias = _bias_lookup(rel_bias, dist)
    bias, valid, valid0 = [], [], []
    for window, d in DILATED_CFGS:
        member = (dist % d == 0) & (dist <= window) & ((PAST_LEN - dist) >= 0)
        bias.append(slot_bias)
        valid.append(jnp.asarray(np.broadcast_to(member.astype(np.float32)[None, :], (N_HEADS, WIN_MAX))))
        valid0.append(jnp.full((N_HEADS, LANES), float(PAST_LEN >= 0), F32))
    bias0 = jnp.broadcast_to(_bias_lookup(rel_bias, np.zeros((1,), np.int64)), (N_HEADS, LANES))
    return jnp.stack(bias), jnp.stack(valid), bias0, jnp.stack(valid0)


def kernel(x_prompt, x_sample, cache_k, cache_v, state_conv, c_prompt, c_sample, rel_bias, w_ada, b_ada, g_norm1,
           w_in, conv_w, conv_b, ln_g, ln_b, w_out, g_norm2, w_router_group, b_router_group, w_router_expert,
           b_router_expert, w_expert_gate, w_expert_up, w_expert_down, g_final):
    depth = w_ada.shape[0]
    assert depth == 1, "single-layer step"
    bp, s, d = x_prompt.shape
    ns = x_sample.shape[0]
    w_in_b = w_in[0].astype(BF16)
    w_out_b = w_out[0].astype(BF16)
    wr_hi, wr_lo, br = router_weights(w_router_group[0], b_router_group[0], w_router_expert[0], b_router_expert[0])

    pad = (-(bp + ns)) % 8
    c_all = jnp.concatenate([c_prompt, c_sample, jnp.zeros((pad, d), c_prompt.dtype)], axis=0)
    mod = ada_modulation(c_all, w_ada[0], b_ada[0])
    mod_p = mod[:bp].reshape(bp, 1, 6 * d)
    sh1, sc1, gt1, sh2, sc2, gt2 = [mod_p[..., j * d:(j + 1) * d] for j in range(6)]
    mod_s = mod[bp:bp + ns]
    ssh1, ssc1, sgt1, ssh2, ssc2, sgt2 = [mod_s[:, j * d:(j + 1) * d] for j in range(6)]

    qs, ks, vs, (kc, vc, conv_act, tail) = prompt_inproj(x_prompt, sc1, sh1, g_norm1[0], w_in_b, conv_w[0], conv_b[0],
                                                         ln_g[0], ln_b[0])
    o_list, lse_list = [], []
    for (window, dil), q, k, v in zip(DILATED_CFGS, qs, ks, vs):
        o, lse = attention_branch(q, k, v, prompt_bias_table(rel_bias, dil), window, dil)
        o_list.append(o)
        lse_list.append(lse)
    x1, h2, route, counts = prompt_outproj(o_list, lse_list, conv_act, x_prompt, gt1, sc2, sh2, g_norm2[0], w_out_b,
                                           wr_hi, wr_lo, br)
    t = bp * s
    nblocks = (2 * t) // MOE_BLK + N_EXPERTS
    assert d == ROW_TILE * LANES
    dest, blk_e, nused, pend_ext = slot_plan(route, counts, nblocks)
    xs = moe_dispatch(h2.reshape(t, d), dest, pend_ext, nblocks * MOE_BLK)
    yb = moe_experts(xs, blk_e, nused, w_expert_gate[0], w_expert_up[0], w_expert_down[0])
    y_prompt = moe_combine(yb, dest, x1, route, gt2, g_final)

    xs2 = x_sample.reshape(ns, d)
    to_feature_major = lambda c: jnp.transpose(c, (0, 2, 3, 1)).reshape(c.shape[0], D_ATT, WIN_MAX)
    from_feature_major = lambda c: jnp.transpose(c.reshape(-1, N_HEADS, HEAD_DIM, WIN_MAX), (0, 3, 1, 2))[None]
    sq, sk, sv, sglu = sample_inproj(xs2, ssc1, ssh1, g_norm1[0], w_in_b)
    ko, vo, att_s = sample_cache_attention(sq, sk, sv, to_feature_major(cache_k[0]), to_feature_major(cache_v[0]),
                                           *sample_bias_tables(rel_bias))
    st_taps = jnp.transpose(state_conv[0], (1, 0, 2))
    y_s = sample_tail(att_s, sglu, st_taps, xs2, sgt1, ssc2, ssh2, sgt2, conv_w[0], conv_b[0], ln_g[0], ln_b[0],
                      w_out_b, g_norm2[0], wr_hi, wr_lo, br, g_final, w_expert_gate[0], w_expert_up[0],
                      w_expert_down[0])
    new_conv_s = jnp.transpose(jnp.concatenate([st_taps[1:], sglu[None]], axis=0), (1, 0, 2))

    return (y_prompt, y_s.reshape(ns, 1, d),
            from_feature_major(kc), from_feature_major(vc), tail[:, CONV_PAD - (CONV_W - 1):][None],
            from_feature_major(ko), from_feature_major(vo), new_conv_s[None])
```

```python
import functools
import math

import numpy as np
import jax
import jax.numpy as jnp
from jax import lax
from jax.experimental import pallas as pl
from jax.experimental.pallas import tpu as pltpu

F32 = jnp.float32
BF16 = jnp.bfloat16

D_MODEL = 1024
HEAD_DIM = 64
N_HEADS = 8
D_ATT = N_HEADS * HEAD_DIM
D_CONV = D_MODEL - D_ATT
DILATED_CFGS = ((128, 1), (512, 4), (2048, 16))
WIN_MAX = max(w for w, _ in DILATED_CFGS)
ATT_BLK = 128
CONV_W = 31
NUM_BUCKETS = 32
MAX_DISTANCE = WIN_MAX
N_GROUPS = 4
EXPERTS_PER_GROUP = 8
N_EXPERTS = N_GROUPS * EXPERTS_PER_GROUP
D_EXPERT = 512
EPS = 1e-6
NEG_INF = -1e30
PAST_LEN = 8192

LANES = 128
ROW_TILE = 8
CONV_PAD = 32
CONV_TAIL = 16
ROUTE_LANES = 128
GROUP_LANE0 = N_EXPERTS
MOE_BLK = 512
VMEM_LIMIT = 56 * 1024 * 1024


def _cparams(sem):
    return pltpu.CompilerParams(dimension_semantics=sem, vmem_limit_bytes=VMEM_LIMIT)


def _sigmoid(x):
    return 1.0 / (1.0 + jnp.exp(-x))


def _silu(x):
    return x * _sigmoid(x)


def _rms_scale(x):
    return x * lax.rsqrt(jnp.mean(x * x, axis=-1, keepdims=True) + EPS)


def _dot(a, b):
    return jnp.dot(a, b, preferred_element_type=F32)


def _ada_body(c_ref, w_ref, b_ref, o_ref):
    s = _silu(c_ref[...]).astype(BF16)
    o_ref[...] = _dot(s, w_ref[...].astype(BF16)) + b_ref[...]


def ada_modulation(c, w_ada, b_ada):
    rows, d = c.shape
    n = w_ada.shape[1]
    tn = 1024
    return pl.pallas_call(
        _ada_body,
        grid=(n // tn,),
        in_specs=[pl.BlockSpec((rows, d), lambda j: (0, 0)),
                  pl.BlockSpec((d, tn), lambda j: (0, j)),
                  pl.BlockSpec((1, tn), lambda j: (0, j))],
        out_specs=pl.BlockSpec((rows, tn), lambda j: (0, j)),
        out_shape=jax.ShapeDtypeStruct((rows, n), F32),
        compiler_params=_cparams(("arbitrary",)),
        name="ada_modulation",
    )(c, w_ada, b_ada.reshape(1, n))


def _conv_ln_swish(cbuf, tm, cw_ref, cb_ref, lg_ref, lb_ref, out_ref):
    chunk = 128
    base0 = CONV_PAD - (CONV_W - 1)
    for c in range(tm // chunk):
        pieces = []
        for lt in range(D_CONV // LANES):
            ls = slice(lt * LANES, (lt + 1) * LANES)
            acc = jnp.broadcast_to(cb_ref[:, ls], (chunk, LANES))
            for phase in range(ROW_TILE):
                part = None
                for j in range(CONV_W):
                    if (base0 + j) % ROW_TILE != phase:
                        continue
                    r0 = c * chunk + base0 + j - phase
                    term = cw_ref[j:j + 1, ls] * cbuf[r0:r0 + chunk + ROW_TILE, ls]
                    part = term if part is None else part + term
                if part is not None:
                    acc = acc + part[phase:phase + chunk, :]
            pieces.append(acc)
        acc = jnp.concatenate(pieces, axis=-1)
        mu = jnp.mean(acc, axis=-1, keepdims=True)
        xc = acc - mu
        var = jnp.mean(xc * xc, axis=-1, keepdims=True)
        y = xc * lax.rsqrt(var + EPS) * lg_ref[...] + lb_ref[...]
        out_ref[0, c * chunk:(c + 1) * chunk, :] = _silu(y).astype(out_ref.dtype)


def _lane_slabs_store(slab, rows):
    for c in range(rows.shape[1] // LANES):
        slab[c] = rows[:, c * LANES:(c + 1) * LANES]


def _store_residue_major(slab_a, slab_b, rows, out_refs):
    tm = rows.shape[0]
    nslab = slab_a.shape[0]
    _lane_slabs_store(slab_a, rows)
    src, prev_d = slab_a, 1
    dilations = [d for _, d in DILATED_CFGS]
    assert dilations[0] == 1 and len(dilations) <= 3
    for d, ref in zip(dilations, out_refs):
        if d == 1:
            ref[0] = rows.astype(ref.dtype)
            continue
        ratio, n = d // prev_d, tm // d
        for r in range(d):
            r_prev, j = r % prev_d, r // prev_d
            base = r_prev * (tm // prev_d) + j
            piece = jnp.concatenate([src[c, pl.ds(base, n, stride=ratio), :] for c in range(nslab)], axis=-1)
            ref[0, r] = piece.astype(ref.dtype)
            if d != dilations[-1]:
                for c in range(nslab):
                    slab_b[c, r * n:(r + 1) * n, :] = piece[:, c * LANES:(c + 1) * LANES]
        src, prev_d = slab_b, d


def _inproj_body(x_ref, sc_ref, sh_ref, g_ref, w_ref, cw_ref, cb_ref, lg_ref, lb_ref,
                 q1, q4, q16, k1, k4, k16, v1, v4, v16, kc_ref, vc_ref, conv_ref, tail_ref, cbuf, slab, slab_b):
    i = pl.program_id(1)
    tm = x_ref.shape[1]

    @pl.when(i == 0)
    def _():
        cbuf[0:CONV_PAD, :] = jnp.zeros((CONV_PAD, D_CONV), F32)
        cbuf[CONV_PAD + tm:, :] = jnp.zeros((CONV_TAIL, D_CONV), F32)

    x = x_ref[0]
    h = (_rms_scale(x) * g_ref[...] * (1.0 + sc_ref[0]) + sh_ref[0]).astype(BF16)

    def col(j):
        return _dot(h, w_ref[:, j * D_ATT:(j + 1) * D_ATT])

    _store_residue_major(slab, slab_b, col(0) * (HEAD_DIM ** -0.5), (q1, q4, q16))
    k = col(1)
    _store_residue_major(slab, slab_b, k, (k1, k4, k16))
    v = col(2)
    _store_residue_major(slab, slab_b, v, (v1, v4, v16))

    kc_ref[0] = k.T
    vc_ref[0] = v.T

    glu = col(3) * _sigmoid(col(4))
    cbuf[CONV_PAD:CONV_PAD + tm, :] = glu
    tail_ref[0] = glu[tm - CONV_PAD:, :]
    _conv_ln_swish(cbuf, tm, cw_ref, cb_ref, lg_ref, lb_ref, conv_ref)
    cbuf[0:CONV_PAD, :] = cbuf[tm:tm + CONV_PAD, :]


def prompt_inproj(x, sc1, sh1, g1, w_in_bf16, conv_w, conv_b, ln_g, ln_b, tm=512):
    b, s, d = x.shape
    nt = s // tm
    first_cache_tile = (s - WIN_MAX) // tm
    row = lambda v: v.reshape(1, -1)
    tok = lambda bi, i: (bi, i, 0)
    per_b = lambda bi, i: (bi, 0, 0)
    cst = lambda bi, i: (0, 0)
    cache = lambda bi, i: (bi, 0, jnp.maximum(i - first_cache_tile, 0))
    qkv_specs, qkv_shapes = [], []
    for _ in range(3):
        for _, dil in DILATED_CFGS:
            if dil == 1:
                qkv_specs.append(pl.BlockSpec((1, tm, D_ATT), tok))
                qkv_shapes.append(jax.ShapeDtypeStruct((b, s, D_ATT), BF16))
            else:
                qkv_specs.append(pl.BlockSpec((1, dil, tm // dil, D_ATT), lambda bi, i: (bi, 0, i, 0)))
                qkv_shapes.append(jax.ShapeDtypeStruct((b, dil, s // dil, D_ATT), BF16))
    outs = pl.pallas_call(
        _inproj_body,
        grid=(b, nt),
        in_specs=[pl.BlockSpec((1, tm, d), tok),
                  pl.BlockSpec((1, 1, d), per_b),
                  pl.BlockSpec((1, 1, d), per_b),
                  pl.BlockSpec((1, d), cst),
                  pl.BlockSpec(w_in_bf16.shape, cst),
                  pl.BlockSpec(conv_w.shape, cst),
                  pl.BlockSpec((1, D_CONV), cst),
                  pl.BlockSpec((1, D_CONV), cst),
                  pl.BlockSpec((1, D_CONV), cst)],
        out_specs=qkv_specs + [pl.BlockSpec((1, D_ATT, tm), cache),
                               pl.BlockSpec((1, D_ATT, tm), cache),
                               pl.BlockSpec((1, tm, D_CONV), tok),
                               pl.BlockSpec((1, CONV_PAD, D_CONV), per_b)],
        out_shape=qkv_shapes + [jax.ShapeDtypeStruct((b, D_ATT, WIN_MAX), F32),
                                jax.ShapeDtypeStruct((b, D_ATT, WIN_MAX), F32),
                                jax.ShapeDtypeStruct((b, s, D_CONV), BF16),
                                jax.ShapeDtypeStruct((b, CONV_PAD, D_CONV), F32)],
        scratch_shapes=[pltpu.VMEM((tm + CONV_PAD + CONV_TAIL, D_CONV), F32),
                        pltpu.VMEM((D_ATT // LANES, tm, LANES), F32),
                        pltpu.VMEM((D_ATT // LANES, tm, LANES), F32)],
        compiler_params=_cparams(("arbitrary", "arbitrary")),
        name="prompt_inproj_conv",
    )(x, sc1, sh1, row(g1), w_in_bf16, conv_w, row(conv_b), row(ln_g), row(ln_b))
    nd = len(DILATED_CFGS)
    return outs[0:nd], outs[nd:2 * nd], outs[2 * nd:3 * nd], outs[3 * nd:]


def _attn_body(q_ref, kc_ref, kp_ref, vc_ref, vp_ref, bias_ref, hmask_ref, o_ref, lse_ref, *, wsub):
    i = pl.program_id(2)
    nq = q_ref.shape[0] // ATT_BLK
    npair = N_HEADS // 2
    qi = lax.broadcasted_iota(jnp.int32, (ATT_BLK, 2 * ATT_BLK), 0)
    kj = lax.broadcasted_iota(jnp.int32, (ATT_BLK, 2 * ATT_BLK), 1)
    rel = qi + ATT_BLK - kj
    band = (rel >= 0) & (rel <= wsub)
    band_first = band & ((kj >= ATT_BLK) | (i > 0))
    low = lax.broadcasted_iota(jnp.int32, (ATT_BLK, LANES), 1) < HEAD_DIM
    for sub in range(nq):
        r0 = sub * ATT_BLK
        valid = band_first if sub == 0 else band
        scores, values = [], []
        for hp in range(npair):
            cs = slice(hp * LANES, (hp + 1) * LANES)
            qp = q_ref[r0:r0 + ATT_BLK, cs]
            if sub == 0:
                kk = jnp.concatenate([kp_ref[:, cs], kc_ref[0:ATT_BLK, cs]], axis=0)
                vv = jnp.concatenate([vp_ref[:, cs], vc_ref[0:ATT_BLK, cs]], axis=0)
            else:
                kk = kc_ref[r0 - ATT_BLK:r0 + ATT_BLK, cs]
                vv = vc_ref[r0 - ATT_BLK:r0 + ATT_BLK, cs]
            values.append(vv)
            for half in range(2):
                s = lax.dot_general(qp * hmask_ref[half], kk, (((1,), (1,)), ((), ())), preferred_element_type=F32)
                scores.append(jnp.where(valid, s + bias_ref[2 * hp + half], NEG_INF))
        s = jnp.concatenate(scores, axis=0)
        m = jnp.max(s, axis=-1, keepdims=True)
        p = jnp.exp(s - m)
        l = jnp.sum(p, axis=-1, keepdims=True)
        p16 = p.astype(BF16)
        inv = 1.0 / l
        lse = m + jnp.log(l)
        for hp in range(npair):
            cs = slice(hp * LANES, (hp + 1) * LANES)
            ra = slice(2 * hp * ATT_BLK, (2 * hp + 1) * ATT_BLK)
            rb = slice((2 * hp + 1) * ATT_BLK, (2 * hp + 2) * ATT_BLK)
            oa = _dot(p16[ra], values[hp]) * inv[ra]
            ob = _dot(p16[rb], values[hp]) * inv[rb]
            o_ref[r0:r0 + ATT_BLK, cs] = jnp.where(low, oa, ob).astype(o_ref.dtype)
            lse_ref[r0:r0 + ATT_BLK, cs] = jnp.where(low, lse[ra], lse[rb])


def attention_branch(q, k, v, bias, window, dilation, qblocks=8):
    c = q.shape[-1]
    b, n = q.shape[0], q.shape[-2]
    wsub = window // dilation
    qblocks = min(qblocks, n // ATT_BLK)
    tq = ATT_BLK * qblocks
    if dilation == 1:
        cur_spec = pl.BlockSpec((None, tq, c), lambda bi, r, i: (bi, i, 0))
        prev_spec = pl.BlockSpec((None, ATT_BLK, c), lambda bi, r, i: (bi, jnp.maximum(i * qblocks - 1, 0), 0))
    else:
        cur_spec = pl.BlockSpec((None, None, tq, c), lambda bi, r, i: (bi, r, i, 0))
        prev_spec = pl.BlockSpec((None, None, ATT_BLK, c),
                                 lambda bi, r, i: (bi, r, jnp.maximum(i * qblocks - 1, 0), 0))
    lane = jnp.arange(LANES)[None, :]
    hmask = jnp.broadcast_to(jnp.stack([lane < HEAD_DIM, lane >= HEAD_DIM]).astype(BF16), (2, ATT_BLK, LANES))
    return pl.pallas_call(
        functools.partial(_attn_body, wsub=wsub),
        grid=(b, dilation, n // tq),
        in_specs=[cur_spec, cur_spec, prev_spec, cur_spec, prev_spec,
                  pl.BlockSpec(bias.shape, lambda bi, r, i: (0, 0, 0)),
                  pl.BlockSpec(hmask.shape, lambda bi, r, i: (0, 0, 0))],
        out_specs=[cur_spec, cur_spec],
        out_shape=[jax.ShapeDtypeStruct(q.shape, BF16), jax.ShapeDtypeStruct(q.shape, F32)],
        compiler_params=_cparams(("arbitrary", "arbitrary", "arbitrary")),
        name=f"attn_branch_d{dilation}",
    )(q, k, k, v, v, bias, hmask)


def _split_bf16(a):
    hi = a.astype(BF16)
    lo = (a - hi.astype(F32)).astype(BF16)
    return hi, lo


def _router_logits(h2, wr_hi_ref, wr_lo_ref, br_ref):
    hi, lo = _split_bf16(h2)
    return _dot(hi, wr_hi_ref[...]) + (_dot(hi, wr_lo_ref[...]) + _dot(lo, wr_hi_ref[...])) + br_ref[...]


def _route(logits):
    rows = logits.shape[0]
    lane = lax.broadcasted_iota(jnp.int32, (rows, ROUTE_LANES), 1)
    lanef = lane.astype(F32)
    big = float(ROUTE_LANES)
    is_g = (lane >= GROUP_LANE0) & (lane < GROUP_LANE0 + N_GROUPS)
    glog = jnp.where(is_g, logits, -jnp.inf)
    gmax = jnp.max(glog, axis=-1, keepdims=True)
    gsum = jnp.sum(jnp.exp(glog - gmax), axis=-1, keepdims=True)
    p_grp = 1.0 / gsum
    glane = jnp.min(jnp.where(glog == gmax, lanef, big), axis=-1, keepdims=True)
    grp = glane - float(GROUP_LANE0)
    in_grp = jnp.floor(lanef * (1.0 / EXPERTS_PER_GROUP)) == grp
    elog = jnp.where(in_grp, logits, -jnp.inf)
    emax = jnp.max(elog, axis=-1, keepdims=True)
    eexp = jnp.exp(elog - emax)
    pe = eexp / jnp.sum(eexp, axis=-1, keepdims=True)
    pe = jnp.where(in_grp, pe, -1.0)
    p1 = jnp.max(pe, axis=-1, keepdims=True)
    e1 = jnp.min(jnp.where(pe == p1, lanef, big), axis=-1, keepdims=True)
    pe2 = jnp.where(lanef == e1, -1.0, pe)
    p2 = jnp.max(pe2, axis=-1, keepdims=True)
    e2 = jnp.min(jnp.where(pe2 == p2, lanef, big), axis=-1, keepdims=True)
    psum = p1 + p2
    return e1, e2, p_grp * p1 / psum, p_grp * p2 / psum


def _position_order(ref, slab, dilation):
    if dilation == 1:
        return ref[0].astype(F32)
    n = ref.shape[2]
    for r in range(dilation):
        piece = ref[0, r].astype(F32)
        for c in range(slab.shape[0]):
            slab[c, pl.ds(r, n, stride=dilation), :] = piece[:, c * LANES:(c + 1) * LANES]
    return jnp.concatenate([slab[c] for c in range(slab.shape[0])], axis=-1)


def _mix_branches(o_refs, lse_refs, slabs):
    slabs = iter(slabs)
    lses = [_position_order(r, None if d == 1 else next(slabs), d) for r, (_, d) in zip(lse_refs, DILATED_CFGS)]
    outs = [_position_order(r, None if d == 1 else next(slabs), d) for r, (_, d) in zip(o_refs, DILATED_CFGS)]
    mx = jnp.maximum(jnp.maximum(lses[0], lses[1]), lses[2])
    ws = [jnp.exp(l - mx) for l in lses]
    numer = sum(w * o for w, o in zip(ws, outs))
    return numer / (ws[0] + ws[1] + ws[2])


def _outproj_body(o1, o4, o16, l1, l4, l16, conv_ref, x_ref, gt_ref, sc_ref, sh_ref, g_ref,
                  wo_ref, wrh_ref, wrl_ref, br_ref,
                  x1_ref, h2_ref, route_ref, cnt_ref, carry, *slabs):
    first = (pl.program_id(0) == 0) & (pl.program_id(1) == 0)

    @pl.when(first)
    def _():
        carry[...] = jnp.zeros_like(carry)

    tm = x_ref.shape[1]
    att = _mix_branches((o1, o4, o16), (l1, l4, l16), slabs).astype(BF16)
    mix = _dot(att, wo_ref[0:D_ATT, :]) + _dot(conv_ref[0], wo_ref[D_ATT:, :])
    x1 = x_ref[0] + gt_ref[0] * mix
    x1_ref[0] = x1
    h2 = _rms_scale(x1) * g_ref[...] * (1.0 + sc_ref[0]) + sh_ref[0]
    h2_ref[0] = h2.astype(BF16)
    e1, e2, w1, w2 = _route(_router_logits(h2, wrh_ref, wrl_ref, br_ref))

    lane = lax.broadcasted_iota(jnp.int32, (tm, ROUTE_LANES), 1).astype(F32)
    oh1 = lane == e1
    oh2 = lane == e2
    onehot = jnp.where(oh1 | oh2, 1.0, 0.0)
    ri = lax.broadcasted_iota(jnp.int32, (tm, tm), 0)
    ci = lax.broadcasted_iota(jnp.int32, (tm, tm), 1)
    tri = jnp.where(ci < ri, 1.0, 0.0).astype(BF16)
    before = _dot(tri, onehot.astype(BF16)) + carry[...]
    r1 = jnp.sum(jnp.where(oh1, before, 0.0), axis=-1, keepdims=True)
    r2 = jnp.sum(jnp.where(oh2, before, 0.0), axis=-1, keepdims=True)
    carry[...] = carry[...] + jnp.sum(onehot, axis=0, keepdims=True)
    cnt_ref[...] = carry[...]
    rec = jnp.where(lane == 0.0, e1, 0.0)
    for idx, val in ((1, e2), (2, w1), (3, w2), (4, r1), (5, r2)):
        rec = jnp.where(lane == float(idx), val, rec)
    route_ref[0] = rec


def prompt_outproj(o_list, lse_list, conv_act, x, gt1, sc2, sh2, g2, w_out_bf16, wr_hi, wr_lo, br, tm=512):
    b, s, d = x.shape
    tok = lambda bi, i: (bi, i, 0)
    per_b = lambda bi, i: (bi, 0, 0)
    cst = lambda bi, i: (0, 0)
    half = pl.BlockSpec((1, tm, D_ATT), tok)
    branch_specs = [half if dil == 1 else pl.BlockSpec((1, dil, tm // dil, D_ATT), lambda bi, i: (bi, 0, i, 0))
                    for _, dil in DILATED_CFGS]
    n_slabs = 2 * sum(1 for _, dil in DILATED_CFGS if dil > 1)
    return pl.pallas_call(
        _outproj_body,
        grid=(b, s // tm),
        in_specs=branch_specs * 2 + [half, pl.BlockSpec((1, tm, d), tok),
                               pl.BlockSpec((1, 1, d), per_b),
                               pl.BlockSpec((1, 1, d), per_b),
                               pl.BlockSpec((1, 1, d), per_b),
                               pl.BlockSpec((1, d), cst),
                               pl.BlockSpec(w_out_bf16.shape, cst),
                               pl.BlockSpec(wr_hi.shape, cst),
                               pl.BlockSpec(wr_lo.shape, cst),
                               pl.BlockSpec((1, ROUTE_LANES), cst)],
        out_specs=[pl.BlockSpec((1, tm, d), tok),
                   pl.BlockSpec((1, tm, d), tok),
                   pl.BlockSpec((1, tm, ROUTE_LANES), tok),
                   pl.BlockSpec((1, ROUTE_LANES), cst)],
        out_shape=[jax.ShapeDtypeStruct((b, s, d), F32),
                   jax.ShapeDtypeStruct((b, s, d), BF16),
                   jax.ShapeDtypeStruct((b, s, ROUTE_LANES), F32),
                   jax.ShapeDtypeStruct((1, ROUTE_LANES), F32)],
        scratch_shapes=[pltpu.VMEM((1, ROUTE_LANES), F32)] + [pltpu.VMEM((D_ATT // LANES, tm, LANES), F32)] * n_slabs,
        compiler_params=_cparams(("arbitrary", "arbitrary")),
        name="prompt_outproj_router",
    )(*o_list, *lse_list, conv_act, x, gt1, sc2, sh2, g2.reshape(1, d), w_out_bf16, wr_hi, wr_lo, br)


def router_weights(w_rg, b_rg, w_re, b_re):
    d = w_rg.shape[0]
    fill = ROUTE_LANES - N_EXPERTS - N_GROUPS
    w = jnp.concatenate([w_re.astype(F32), w_rg.astype(F32), jnp.zeros((d, fill), F32)], axis=1)
    bias = jnp.concatenate([b_re.astype(F32), b_rg.astype(F32), jnp.zeros((fill,), F32)])[None, :]
    hi = w.astype(BF16)
    lo = (w - hi.astype(F32)).astype(BF16)
    return hi, lo, bias


def _rows_from_tiles(tile_ref, n):
    return jnp.concatenate([tile_ref[pl.ds(c, n, stride=ROW_TILE), :] for c in range(ROW_TILE)], axis=-1)


def _rows_to_tiles(tile_ref, rows):
    n = rows.shape[0]
    for c in range(ROW_TILE):
        tile_ref[pl.ds(c, n, stride=ROW_TILE), :] = rows[:, c * LANES:(c + 1) * LANES]


def _dispatch_body(dest_ref, prev_ref, pend_ref, h2_ref, xs_hbm, tiles, zbuf, sems, zsem, *, chunk, nsteps):
    step = pl.program_id(0)

    @pl.when(step == 0)
    def _():
        zbuf[...] = jnp.zeros_like(zbuf)

        def zcopy(e):
            start = pl.multiple_of(jnp.maximum(pend_ref[e + 1] - MOE_BLK * ROW_TILE, 0), ROW_TILE)
            return pltpu.make_async_copy(zbuf, xs_hbm.at[pl.ds(start, MOE_BLK * ROW_TILE)], zsem)

        def zstart(e, c):
            @pl.when(pend_ref[e + 1] > pend_ref[e])
            def _():
                zcopy(e).start()
            return c

        def zwait(e, c):
            @pl.when(pend_ref[e + 1] > pend_ref[e])
            def _():
                zcopy(e).wait()
            return c

        lax.fori_loop(0, N_EXPERTS, zstart, 0)
        lax.fori_loop(0, N_EXPERTS, zwait, 0)

        blk_rows = MOE_BLK * ROW_TILE
        nblk = xs_hbm.shape[0] // blk_rows

        def tcopy(i):
            return pltpu.make_async_copy(zbuf, xs_hbm.at[pl.ds(pl.multiple_of(i * blk_rows, blk_rows), blk_rows)],
                                         zsem)

        def tstart(i, c):
            @pl.when(i * blk_rows >= pend_ref[N_EXPERTS])
            def _():
                tcopy(i).start()
            return c

        def twait(i, c):
            @pl.when(i * blk_rows >= pend_ref[N_EXPERTS])
            def _():
                tcopy(i).wait()
            return c

        lax.fori_loop(0, nblk, tstart, 0)
        lax.fori_loop(0, nblk, twait, 0)

    def copy(dref, slot, t, k):
        src = pl.multiple_of(t * ROW_TILE, ROW_TILE)
        dst = pl.multiple_of(dref[0, 0, k * chunk + t], ROW_TILE)
        return pltpu.make_async_copy(tiles.at[slot, pl.ds(src, ROW_TILE)], xs_hbm.at[pl.ds(dst, ROW_TILE)],
                                     sems.at[slot])

    def issue(dref, slot):
        def body(t, c):
            copy(dref, slot, t, 0).start(priority=0)
            copy(dref, slot, t, 1).start(priority=1)
            return c
        lax.fori_loop(0, chunk, body, 0, unroll=8)

    def drain(dref, slot):
        def body(t, c):
            copy(dref, slot, t, 0).wait()
            copy(dref, slot, t, 1).wait()
            return c
        lax.fori_loop(0, chunk, body, 0, unroll=8)

    for slot in range(2):
        @pl.when(lax.rem(step, 2) == slot)
        def _(slot=slot):
            _rows_to_tiles(tiles.at[slot], h2_ref[...].astype(F32))
            issue(dest_ref, slot)

            @pl.when(step > 0)
            def _():
                drain(prev_ref, 1 - slot)

            @pl.when(step == nsteps - 1)
            def _():
                drain(dest_ref, slot)


def moe_dispatch(h2_flat, dest_rt, pends_rt, nslots, chunk=1024):
    t, d = h2_flat.shape
    assert d == ROW_TILE * LANES
    nsteps = t // chunk
    dest3 = jnp.transpose(dest_rt.reshape(2, nsteps, chunk), (1, 0, 2)).reshape(nsteps, 1, 2 * chunk)
    return pl.pallas_call(
        functools.partial(_dispatch_body, chunk=chunk, nsteps=nsteps),
        grid=(nsteps,),
        in_specs=[pl.BlockSpec((1, 1, 2 * chunk), lambda i: (i, 0, 0), memory_space=pltpu.SMEM),
                  pl.BlockSpec((1, 1, 2 * chunk), lambda i: (jnp.maximum(i - 1, 0), 0, 0), memory_space=pltpu.SMEM),
                  pl.BlockSpec(memory_space=pltpu.SMEM),
                  pl.BlockSpec((chunk, d), lambda i: (i, 0))],
        out_specs=pl.BlockSpec(memory_space=pl.ANY),
        out_shape=jax.ShapeDtypeStruct((nslots * ROW_TILE, LANES), F32),
        scratch_shapes=[pltpu.VMEM((2, chunk * ROW_TILE, LANES), F32),
                        pltpu.VMEM((MOE_BLK * ROW_TILE, LANES), F32),
                        pltpu.SemaphoreType.DMA((2,)),
                        pltpu.SemaphoreType.DMA(())],
        compiler_params=_cparams(("arbitrary",)),
        name="moe_dispatch",
    )(dest3, dest3, pends_rt, h2_flat)


def _experts_body(blk_e_ref, nused_ref, xs_ref, w1_ref, w3_ref, w2_ref, y_ref, w1b, w3b, w2b):
    i = pl.program_id(0)
    new_expert = (i == 0) | (blk_e_ref[i] != blk_e_ref[jnp.maximum(i - 1, 0)])

    @pl.when((i < nused_ref[0]) & new_expert)
    def _():
        w1b[...] = w1_ref[0].astype(BF16)
        w3b[...] = w3_ref[0].astype(BF16)
        w2b[...] = w2_ref[0].astype(BF16)

    @pl.when(i < nused_ref[0])
    def _():
        xb = _rows_from_tiles(xs_ref, MOE_BLK).astype(BF16)
        a = (_silu(_dot(xb, w1b[...])) * _dot(xb, w3b[...])).astype(BF16)
        _rows_to_tiles(y_ref, _dot(a, w2b[...]))

    @pl.when(i >= nused_ref[0])
    def _():
        y_ref[...] = jnp.zeros_like(y_ref)


def moe_experts(xs, blk_e, nused, w_e1, w_e3, w_e2):
    nb = xs.shape[0] // (MOE_BLK * ROW_TILE)
    _, d, de = w_e1.shape
    blk = lambda i, be, nu: (jnp.minimum(i, nu[0] - 1), 0)
    wsel = lambda i, be, nu: (be[i], 0, 0)
    grid_spec = pltpu.PrefetchScalarGridSpec(
        num_scalar_prefetch=2,
        grid=(nb,),
        in_specs=[pl.BlockSpec((MOE_BLK * ROW_TILE, LANES), blk),
                  pl.BlockSpec((1, d, de), wsel),
                  pl.BlockSpec((1, d, de), wsel),
                  pl.BlockSpec((1, de, d), wsel)],
        out_specs=pl.BlockSpec((MOE_BLK * ROW_TILE, LANES), lambda i, be, nu: (i, 0)),
        scratch_shapes=[pltpu.VMEM((d, de), BF16), pltpu.VMEM((d, de), BF16), pltpu.VMEM((de, d), BF16)],
    )
    return pl.pallas_call(
        _experts_body,
        grid_spec=grid_spec,
        out_shape=jax.ShapeDtypeStruct(xs.shape, F32),
        compiler_params=_cparams(("arbitrary",)),
        name="moe_experts",
    )(blk_e, nused, xs, w_e1, w_e3, w_e2)


def _combine_body(dest_ref, next_ref, yb_hbm, x1_ref, route_ref, gt_ref, gf_ref, out_ref, ya0, ya1, sems,
                  *, tm, nsteps):
    step = pl.program_id(0) * pl.num_programs(1) + pl.program_id(1)

    def copy(dref, slot, t, k):
        src = pl.multiple_of(dref[0, 0, k * tm + t], ROW_TILE)
        dst = pl.multiple_of(t * ROW_TILE, ROW_TILE)
        buf = ya0 if k == 0 else ya1
        return pltpu.make_async_copy(yb_hbm.at[pl.ds(src, ROW_TILE)], buf.at[slot, pl.ds(dst, ROW_TILE)],
                                     sems.at[slot])

    def issue(dref, slot):
        def body(t, c):
            copy(dref, slot, t, 0).start(priority=0)
            copy(dref, slot, t, 1).start(priority=1)
            return c
        lax.fori_loop(0, tm, body, 0, unroll=8)

    def drain(dref, slot):
        def body(t, c):
            copy(dref, slot, t, 0).wait()
            copy(dref, slot, t, 1).wait()
            return c
        lax.fori_loop(0, tm, body, 0, unroll=8)

    for slot in range(2):
        @pl.when(lax.rem(step, 2) == slot)
        def _(slot=slot):
            @pl.when(step == 0)
            def _():
                issue(dest_ref, slot)

            @pl.when(step + 1 < nsteps)
            def _():
                issue(next_ref, 1 - slot)

            drain(dest_ref, slot)
            rec = route_ref[0]
            moe = (rec[:, 2:3] * _rows_from_tiles(ya0.at[slot], tm)
                   + rec[:, 3:4] * _rows_from_tiles(ya1.at[slot], tm))
            y = x1_ref[0] + gt_ref[0] * moe
            out_ref[0] = _rms_scale(y) * gf_ref[...]


def moe_combine(yb, dest_rt, x1, route, gt2, g_final, tm=512):
    b, s, d = x1.shape
    nt = s // tm
    nsteps = b * nt
    dest3 = jnp.transpose(dest_rt.reshape(2, nsteps, tm), (1, 0, 2)).reshape(nsteps, 1, 2 * tm)
    tok = lambda bi, i: (bi, i, 0)
    return pl.pallas_call(
        functools.partial(_combine_body, tm=tm, nsteps=nsteps),
        grid=(b, nt),
        in_specs=[pl.BlockSpec((1, 1, 2 * tm), lambda bi, i: (bi * nt + i, 0, 0), memory_space=pltpu.SMEM),
                  pl.BlockSpec((1, 1, 2 * tm), lambda bi, i: (jnp.minimum(bi * nt + i + 1, nsteps - 1), 0, 0),
                               memory_space=pltpu.SMEM),
                  pl.BlockSpec(memory_space=pl.ANY),
                  pl.BlockSpec((1, tm, d), tok),
                  pl.BlockSpec((1, tm, ROUTE_LANES), tok),
                  pl.BlockSpec((1, 1, d), lambda bi, i: (bi, 0, 0)),
                  pl.BlockSpec((1, d), lambda bi, i: (0, 0))],
        out_specs=pl.BlockSpec((1, tm, d), tok),
        out_shape=jax.ShapeDtypeStruct((b, s, d), F32),
        scratch_shapes=[pltpu.VMEM((2, tm * ROW_TILE, LANES), F32), pltpu.VMEM((2, tm * ROW_TILE, LANES), F32),
                        pltpu.SemaphoreType.DMA((2,))],
        compiler_params=_cparams(("arbitrary", "arbitrary")),
        name="moe_combine_final",
    )(dest3, dest3, yb, x1, route, gt2, g_final.reshape(1, d))


def slot_plan(route, counts, nblocks):
    cnt = counts[0, :N_EXPERTS].astype(jnp.int32)
    padded = (cnt + MOE_BLK - 1) // MOE_BLK * MOE_BLK
    pends = jnp.cumsum(padded)
    pstarts = pends - padded
    experts = jnp.arange(N_EXPERTS, dtype=jnp.int32)
    dests = []
    for k in range(2):
        eid = route[..., k].reshape(-1).astype(jnp.int32)
        rank = route[..., 4 + k].reshape(-1).astype(jnp.int32)
        dests.append(jnp.sum(jnp.where(eid[:, None] == experts, pstarts, 0), axis=-1) + rank)
    dest = jnp.stack(dests)
    blk_start = jnp.arange(nblocks, dtype=jnp.int32) * MOE_BLK
    blk_e = jnp.minimum(jnp.sum((pends[None, :] <= blk_start[:, None]).astype(jnp.int32), axis=1), N_EXPERTS - 1)
    nused = (pends[-1:] // MOE_BLK).astype(jnp.int32)
    pend_ext = jnp.concatenate([jnp.zeros((1,), jnp.int32), pends.astype(jnp.int32)])
    return (dest * ROW_TILE).astype(jnp.int32), blk_e, nused, (pend_ext * ROW_TILE).astype(jnp.int32)


def _sample_inproj_body(x_ref, sc_ref, sh_ref, g_ref, w_ref, q_ref, k_ref, v_ref, glu_ref):
    h = (_rms_scale(x_ref[...]) * g_ref[...] * (1.0 + sc_ref[...]) + sh_ref[...]).astype(BF16)

    def col(j):
        return _dot(h, w_ref[:, j * D_ATT:(j + 1) * D_ATT])

    q_ref[...] = col(0) * (HEAD_DIM ** -0.5)
    k_ref[...] = col(1)
    v_ref[...] = col(2)
    glu_ref[...] = col(3) * _sigmoid(col(4))


def sample_inproj(x, sc1, sh1, g1, w_in_bf16):
    n, d = x.shape
    out = jax.ShapeDtypeStruct((n, D_ATT), F32)
    return pl.pallas_call(
        _sample_inproj_body,
        out_shape=[out, out, out, out],
        compiler_params=pltpu.CompilerParams(vmem_limit_bytes=VMEM_LIMIT),
        name="sample_inproj",
    )(x, sc1, sh1, g1.reshape(1, d), w_in_bf16)


def _sample_cache_body(qt_ref, knt_ref, vnt_ref, k_ref, v_ref, bias_ref, valid_ref, bias0_ref, valid0_ref,
                       ko_ref, vo_ref, att_ref, s_s, p_s, n_s):
    b = pl.program_id(0)
    win = k_ref.shape[2]
    sel = lax.broadcasted_iota(jnp.int32, qt_ref.shape, 1) == b

    def column(ref):
        return jnp.sum(jnp.where(sel, ref[...], 0.0), axis=1, keepdims=True)

    q, kn, vn = column(qt_ref), column(knt_ref), column(vnt_ref)
    kt = k_ref[0]
    vt = v_ref[0]
    for h in range(N_HEADS):
        rows = slice(h * HEAD_DIM, (h + 1) * HEAD_DIM)
        s_s[h:h + 1, :] = jnp.sum(kt[rows, :] * q[rows, :], axis=0, keepdims=True)
        n_s[h:h + 1, :] = jnp.broadcast_to(jnp.sum(kn[rows, :] * q[rows, :], axis=0, keepdims=True), (1, LANES))
    s_all = s_s[...]
    s_new = n_s[...][:, 0:1] + bias0_ref[:, 0:1]
    parts = []
    for br in range(len(DILATED_CFGS)):
        s = jnp.where(valid_ref[br] > 0.0, s_all + bias_ref[br], NEG_INF)
        s0 = jnp.where(valid0_ref[br][:, 0:1] > 0.0, s_new, NEG_INF)
        m = jnp.maximum(jnp.max(s, axis=-1, keepdims=True), s0)
        p = jnp.exp(s - m)
        p0 = jnp.exp(s0 - m)
        parts.append((p, p0, m, jnp.sum(p, axis=-1, keepdims=True) + p0))
    mx = jnp.maximum(jnp.maximum(parts[0][2], parts[1][2]), parts[2][2])
    ws = [jnp.exp(m - mx) for _, _, m, _ in parts]
    inv = 1.0 / sum(w * l for w, (_, _, _, l) in zip(ws, parts))
    p_s[...] = sum(w * p for w, (p, _, _, _) in zip(ws, parts)) * inv
    p_new = sum(w * p0 for w, (_, p0, _, _) in zip(ws, parts)) * inv
    cols = []
    for h in range(N_HEADS):
        rows = slice(h * HEAD_DIM, (h + 1) * HEAD_DIM)
        cols.append(jnp.sum(vt[rows, :] * p_s[h:h + 1, :], axis=1, keepdims=True) + p_new[h:h + 1, :] * vn[rows, :])
    att = jnp.concatenate(cols, axis=0)

    @pl.when(b == 0)
    def _():
        att_ref[...] = jnp.zeros_like(att_ref)

    att_ref[...] = jnp.where(sel, att, att_ref[...])
    last = lax.broadcasted_iota(jnp.int32, kt.shape, 1) == win - 1
    ko_ref[0] = jnp.where(last, kn, pltpu.roll(kt, win - 1, 1))
    vo_ref[0] = jnp.where(last, vn, pltpu.roll(vt, win - 1, 1))


def sample_cache_attention(q, k_new, v_new, cache_kt, cache_vt, bias, valid, bias0, valid0):
    n, c, win = cache_kt.shape
    full = lambda a: pl.BlockSpec(a.shape, lambda i: (0,) * a.ndim)
    seq = pl.BlockSpec((1, c, win), lambda i: (i, 0, 0))
    qt, knt, vnt = q.T, k_new.T, v_new.T
    ko, vo, att_t = pl.pallas_call(
        _sample_cache_body,
        grid=(n,),
        in_specs=[full(qt), full(knt), full(vnt), seq, seq, full(bias), full(valid), full(bias0), full(valid0)],
        out_specs=[seq, seq, pl.BlockSpec((c, n), lambda i: (0, 0))],
        out_shape=[jax.ShapeDtypeStruct(cache_kt.shape, F32), jax.ShapeDtypeStruct(cache_vt.shape, F32),
                   jax.ShapeDtypeStruct((c, n), F32)],
        scratch_shapes=[pltpu.VMEM((N_HEADS, win), F32), pltpu.VMEM((N_HEADS, win), F32),
                        pltpu.VMEM((N_HEADS, LANES), F32)],
        compiler_params=_cparams(("arbitrary",)),
        name="sample_cache_attention",
    )(qt, knt, vnt, cache_kt, cache_vt, bias, valid, bias0, valid0)
    return ko, vo, att_t.T


def _sample_tail_body(att_ref, glu_ref, st_ref, x_ref, gt1_ref, sc2_ref, sh2_ref, gt2_ref,
                      cw_ref, cb_ref, lg_ref, lb_ref, wo_ref, g2_ref, wrh_ref, wrl_ref, br_ref, gf_ref,
                      w1_ref, w3_ref, w2_ref, y_ref, x1_s, h2_s, gate_s, acc_s):
    e = pl.program_id(0)
    n = x_ref.shape[0]

    @pl.when(e == 0)
    def _():
        nhist = CONV_W - 1
        conv = cw_ref[nhist:CONV_W, :] * glu_ref[...] + cb_ref[...]
        for j in range(nhist):
            conv = conv + cw_ref[j:j + 1, :] * st_ref[j]
        mu = jnp.mean(conv, axis=-1, keepdims=True)
        xc = conv - mu
        var = jnp.mean(xc * xc, axis=-1, keepdims=True)
        act = _silu(xc * lax.rsqrt(var + EPS) * lg_ref[...] + lb_ref[...]).astype(BF16)
        mix = _dot(att_ref[...].astype(BF16), wo_ref[0:D_ATT, :]) + _dot(act, wo_ref[D_ATT:, :])
        x1 = x_ref[...] + gt1_ref[...] * mix
        x1_s[...] = x1
        h2 = _rms_scale(x1) * g2_ref[...] * (1.0 + sc2_ref[...]) + sh2_ref[...]
        h2_s[...] = h2.astype(BF16)
        e1, e2, w1, w2 = _route(_router_logits(h2, wrh_ref, wrl_ref, br_ref))
        lane = lax.broadcasted_iota(jnp.int32, (n, ROUTE_LANES), 1).astype(F32)
        gate_s[...] = jnp.where(lane == e1, w1, 0.0) + jnp.where(lane == e2, w2, 0.0)
        acc_s[...] = jnp.zeros_like(acc_s)

    hb = h2_s[...]
    a = (_silu(_dot(hb, w1_ref[0].astype(BF16))) * _dot(hb, w3_ref[0].astype(BF16))).astype(BF16)
    ye = _dot(a, w2_ref[0].astype(BF16))
    lane = lax.broadcasted_iota(jnp.int32, (n, ROUTE_LANES), 1)
    g = jnp.sum(jnp.where(lane == e, gate_s[...], 0.0), axis=-1, keepdims=True)
    acc_s[...] = acc_s[...] + g * ye

    @pl.when(e == pl.num_programs(0) - 1)
    def _():
        y = x1_s[...] + gt2_ref[...] * acc_s[...]
        y_ref[...] = _rms_scale(y) * gf_ref[...]


def sample_tail(att, glu, state_conv, x, gt1, sc2, sh2, gt2, conv_w, conv_b, ln_g, ln_b, w_out_bf16, g2,
                wr_hi, wr_lo, br, g_final, w_e1, w_e3, w_e2):
    n, d = x.shape
    de = w_e1.shape[2]
    row = lambda v: v.reshape(1, -1)
    full = lambda a: pl.BlockSpec(a.shape, lambda e: (0,) * a.ndim)
    args = [att, glu, state_conv, x, gt1, sc2, sh2, gt2, conv_w, row(conv_b), row(ln_g), row(ln_b), w_out_bf16,
            row(g2), wr_hi, wr_lo, br, row(g_final)]
    return pl.pallas_call(
        _sample_tail_body,
        grid=(N_EXPERTS,),
        in_specs=[full(a) for a in args] + [pl.BlockSpec((1, d, de), lambda e: (e, 0, 0)),
                                            pl.BlockSpec((1, d, de), lambda e: (e, 0, 0)),
                                            pl.BlockSpec((1, de, d), lambda e: (e, 0, 0))],
        out_specs=pl.BlockSpec((n, d), lambda e: (0, 0)),
        out_shape=jax.ShapeDtypeStruct((n, d), F32),
        scratch_shapes=[pltpu.VMEM((n, d), F32), pltpu.VMEM((n, d), BF16),
                        pltpu.VMEM((n, ROUTE_LANES), F32), pltpu.VMEM((n, d), F32)],
        compiler_params=_cparams(("arbitrary",)),
        name="sample_tail_moe",
    )(*args, w_e1, w_e3, w_e2)


def _t5_bucket_static(dist):
    n = np.maximum(np.asarray(dist, np.int64), 0)
    max_exact = NUM_BUCKETS // 2
    x = np.log(np.maximum(n, 1) / max_exact) / math.log(MAX_DISTANCE / max_exact) * (NUM_BUCKETS - max_exact)
    on_edge = (np.abs(x - np.rint(x)) < 2e-5) & (x > 0.5) & (np.rint(x) < NUM_BUCKETS - max_exact)
    assert not on_edge.any()
    large = np.minimum(max_exact + np.floor(x + 1e-9).astype(np.int64), NUM_BUCKETS - 1)
    return np.where(n < max_exact, n, large)


def _bias_lookup(rel_bias, dist):
    bucket = _t5_bucket_static(dist)
    onehot = (bucket.reshape(-1, 1) == np.arange(NUM_BUCKETS)[None, :]).astype(np.float32)
    table = jnp.einsum('nb,bh->hn', onehot, rel_bias.astype(F32), precision=lax.Precision.HIGHEST)
    return table.reshape((rel_bias.shape[1],) + bucket.shape)


def prompt_bias_table(rel_bias, dilation):
    i = np.arange(ATT_BLK)[:, None]
    j = np.arange(2 * ATT_BLK)[None, :]
    return _bias_lookup(rel_bias, (i + ATT_BLK - j) * dilation)


def sample_bias_tables(rel_bias):
    dist = WIN_MAX - np.arange(WIN_MAX)
    slot_bias = _bias_lookup(rel_bias, dist)
    bias, valid, valid0 = [], [], []
    for window, d in DILATED_CFGS:
        member = (dist % d == 0) & (dist <= window) & ((PAST_LEN - dist) >= 0)
        bias.append(slot_bias)
        valid.append(jnp.asarray(np.broadcast_to(member.astype(np.float32)[None, :], (N_HEADS, WIN_MAX))))
        valid0.append(jnp.full((N_HEADS, LANES), float(PAST_LEN >= 0), F32))
    bias0 = jnp.broadcast_to(_bias_lookup(rel_bias, np.zeros((1,), np.int64)), (N_HEADS, LANES))
    return jnp.stack(bias), jnp.stack(valid), bias0, jnp.stack(valid0)


def kernel(x_prompt, x_sample, cache_k, cache_v, state_conv, c_prompt, c_sample, rel_bias, w_ada, b_ada, g_norm1,
           w_in, conv_w, conv_b, ln_g, ln_b, w_out, g_norm2, w_router_group, b_router_group, w_router_expert,
           b_router_expert, w_expert_gate, w_expert_up, w_expert_down, g_final):
    depth = w_ada.shape[0]
    assert depth == 1, "single-layer step"
    bp, s, d = x_prompt.shape
    ns = x_sample.shape[0]
    w_in_b = w_in[0].astype(BF16)
    w_out_b = w_out[0].astype(BF16)
    wr_hi, wr_lo, br = router_weights(w_router_group[0], b_router_group[0], w_router_expert[0], b_router_expert[0])

    pad = (-(bp + ns)) % 8
    c_all = jnp.concatenate([c_prompt, c_sample, jnp.zeros((pad, d), c_prompt.dtype)], axis=0)
    mod = ada_modulation(c_all, w_ada[0], b_ada[0])
    mod_p = mod[:bp].reshape(bp, 1, 6 * d)
    sh1, sc1, gt1, sh2, sc2, gt2 = [mod_p[..., j * d:(j + 1) * d] for j in range(6)]
    mod_s = mod[bp:bp + ns]
    ssh1, ssc1, sgt1, ssh2, ssc2, sgt2 = [mod_s[:, j * d:(j + 1) * d] for j in range(6)]

    qs, ks, vs, (kc, vc, conv_act, tail) = prompt_inproj(x_prompt, sc1, sh1, g_norm1[0], w_in_b, conv_w[0], conv_b[0],
                                                         ln_g[0], ln_b[0])
    o_list, lse_list = [], []
    for (window, dil), q, k, v in zip(DILATED_CFGS, qs, ks, vs):
        o, lse = attention_branch(q, k, v, prompt_bias_table(rel_bias, dil), window, dil)
        o_list.append(o)
        lse_list.append(lse)
    x1, h2, route, counts = prompt_outproj(o_list, lse_list, conv_act, x_prompt, gt1, sc2, sh2, g_norm2[0], w_out_b,
                                           wr_hi, wr_lo, br)
    t = bp * s
    nblocks = (2 * t) // MOE_BLK + N_EXPERTS
    assert d == ROW_TILE * LANES
    dest, blk_e, nused, pend_ext = slot_plan(route, counts, nblocks)
    xs = moe_dispatch(h2.reshape(t, d), dest, pend_ext, nblocks * MOE_BLK)
    yb = moe_experts(xs, blk_e, nused, w_expert_gate[0], w_expert_up[0], w_expert_down[0])
    y_prompt = moe_combine(yb, dest, x1, route, gt2, g_final)

    xs2 = x_sample.reshape(ns, d)
    to_feature_major = lambda c: jnp.transpose(c, (0, 2, 3, 1)).reshape(c.shape[0], D_ATT, WIN_MAX)
    from_feature_major = lambda c: jnp.transpose(c.reshape(-1, N_HEADS, HEAD_DIM, WIN_MAX), (0, 3, 1, 2))[None]
    sq, sk, sv, sglu = sample_inproj(xs2, ssc1, ssh1, g_norm1[0], w_in_b)
    ko, vo, att_s = sample_cache_attention(sq, sk, sv, to_feature_major(cache_k[0]), to_feature_major(cache_v[0]),
                                           *sample_bias_tables(rel_bias))
    st_taps = jnp.transpose(state_conv[0], (1, 0, 2))
    y_s = sample_tail(att_s, sglu, st_taps, xs2, sgt1, ssc2, ssh2, sgt2, conv_w[0], conv_b[0], ln_g[0], ln_b[0],
                      w_out_b, g_norm2[0], wr_hi, wr_lo, br, g_final, w_expert_gate[0], w_expert_up[0],
                      w_expert_down[0])
    new_conv_s = jnp.transpose(jnp.concatenate([st_taps[1:], sglu[None]], axis=0), (1, 0, 2))

    return (y_prompt, y_s.reshape(ns, 1, d),
            from_feature_major(kc), from_feature_major(vc), tail[:, CONV_PAD - (CONV_W - 1):][None],
            from_feature_major(ko), from_feature_major(vo), new_conv_s[None])
```
